```python
import jax, jax.numpy as jnp
from jax import lax
import numpy as np

D_MODEL = 1024
BATCH = 8
SEQ = 2048
DEPTH = 4
DEC_BATCH = 128
DEC_SEQ = 4
PAST_LEN = 2048
PAGE_SIZE = 128

CHUNK = 128
A_GROUPS = 4
A_GROUP_W = 128
A_WIDTH = A_GROUPS * A_GROUP_W
DIL_PAIRS = ((128, 1), (512, 4), (2048, 16))
N_DIL = len(DIL_PAIRS)
HEADS_PER_GROUP = 4
HEAD_DIM = 64
B_GROUP_W = HEADS_PER_GROUP * HEAD_DIM
B_QKV_W = 3 * N_DIL * B_GROUP_W
BAND_BLOCK = 128
D_FF = 2816
CONV_W = 3
IN_W = 2 * A_WIDTH + B_QKV_W + 2 * D_MODEL
EPS = 1e-6
NEG = -1e30

kernel_name = 'hybrid_gmlp_dilated_swa_convglu_step'


def rms_norm(x, g):
    xf = x.astype(jnp.float32)
    y = xf * lax.rsqrt(jnp.mean(xf * xf, axis=-1, keepdims=True) + EPS)
    return (y * g.astype(jnp.float32)).astype(x.dtype)


def layer_norm(x, g, b):
    xf = x.astype(jnp.float32)
    mu = jnp.mean(xf, axis=-1, keepdims=True)
    var = jnp.mean(jnp.square(xf - mu), axis=-1, keepdims=True)
    y = (xf - mu) * lax.rsqrt(var + EPS)
    return (y * g.astype(jnp.float32) + b.astype(jnp.float32)).astype(x.dtype)


def spatial_gate(u, vn, w_s, b_s):
    N, T, _ = u.shape
    L = min(T, CHUNK)
    nc = T // L
    w = jnp.tril(w_s[:, :L, :L])
    vg = vn.reshape(N, nc, L, A_GROUPS, A_GROUP_W)
    mix = jnp.einsum('gts,bnsgc->bntgc', w, vg) + b_s[:, :L].T[None, None, :, :, None]
    return jax.nn.gelu(u) * mix.reshape(N, T, A_WIDTH)


def dilated_band_attn(q, k, v, dil, span):
    N, T, H, Dh = q.shape
    n = T // dil
    nb = -(-n // BAND_BLOCK)
    npad = nb * BAND_BLOCK

    def to_sub(a):
        a = a.reshape(N, n, dil, H, Dh).transpose(0, 2, 1, 3, 4)
        return jnp.pad(a, ((0, 0), (0, 0), (0, npad - n), (0, 0), (0, 0)))

    def band(a):
        a = jnp.pad(a, ((0, 0), (0, 0), (BAND_BLOCK, 0), (0, 0), (0, 0)))
        a = a.reshape(N, dil, nb + 1, BAND_BLOCK, H, Dh)
        return jnp.concatenate([a[:, :, :-1], a[:, :, 1:]], axis=3)

    qb = to_sub(q).reshape(N, dil, nb, BAND_BLOCK, H, Dh)
    kb = band(to_sub(k))
    vb = band(to_sub(v))
    s = jnp.einsum('brnqhd,brnkhd->brnhqk', qb, kb).astype(jnp.float32) * (HEAD_DIM ** -0.5)
    qi = jnp.arange(BAND_BLOCK)[:, None]
    ki = jnp.arange(2 * BAND_BLOCK)[None, :]
    dist = qi + BAND_BLOCK - ki
    blk = jnp.arange(nb)[:, None, None]
    valid = (dist >= 0) & (dist <= span) & (blk * BAND_BLOCK + ki - BAND_BLOCK >= 0)
    s = jnp.where(valid[None, None, :, None], s, NEG)
    lse = jax.nn.logsumexp(s, axis=-1)
    p = jnp.exp(s - lse[..., None])
    o = jnp.einsum('brnhqk,brnkhd->brnqhd', p.astype(v.dtype), vb)
    o = o.reshape(N, dil, npad, H, Dh)[:, :, :n].transpose(0, 2, 1, 3, 4).reshape(N, T, H, Dh)
    lse = lse.transpose(0, 1, 2, 4, 3).reshape(N, dil, npad, H)[:, :, :n]
    lse = lse.transpose(0, 2, 1, 3).reshape(N, T, H)
    return o, lse


def dilated_cached_attn(q, k_new, v_new, kv_buf, dil, span):
    N, Tn, H, Dh = q.shape
    Wb = kv_buf.shape[1]
    keys = jnp.concatenate([kv_buf[:, :, 0], k_new], axis=1)
    vals = jnp.concatenate([kv_buf[:, :, 1], v_new], axis=1)
    idx = Wb + jnp.arange(Tn)[:, None] - dil * jnp.arange(span + 1)[None, :]
    valid = idx >= 0
    idx = jnp.maximum(idx, 0)
    kg = keys[:, idx]
    vg = vals[:, idx]
    s = jnp.einsum('bqhd,bqjhd->bhqj', q, kg).astype(jnp.float32) * (HEAD_DIM ** -0.5)
    s = jnp.where(valid[None, None], s, NEG)
    lse = jax.nn.logsumexp(s, axis=-1)
    p = jnp.exp(s - lse[..., None])
    o = jnp.einsum('bhqj,bqjhd->bqhd', p.astype(v_new.dtype), vg)
    return o, lse.transpose(0, 2, 1)


def combine_dilations(outs, lses):
    w = jax.nn.softmax(jnp.stack(lses, axis=0), axis=0)
    o = jnp.stack(outs, axis=0)
    return jnp.einsum('gnth,gnthd->nthd', w.astype(o.dtype), o)


def conv_glu(h, w_up, conv_w, conv_b, w_down, conv_state):
    T = h.shape[1]
    up = h @ w_up
    gate, val = up[..., :D_FF], up[..., D_FF:]
    hp = jnp.concatenate([conv_state, gate], axis=1)
    conv = conv_b + sum(conv_w[j] * hp[:, j:j + T] for j in range(CONV_W))
    out = (jax.nn.gelu(conv) * val) @ w_down
    return out, hp[:, -(CONV_W - 1):]


def trunk_layer(x, c, lw, kv_bufs, conv_state):
    (ada_w, ada_b, norm_g, w_in, ln_v_g, ln_v_b, w_spatial, b_spatial,
     w_a2d, w_b2d, w_out, w_up, conv_w, conv_b, w_down) = lw
    N, T, _ = x.shape
    mod = (jax.nn.silu(c) @ ada_w + ada_b)[:, None, :]
    sh1, sc1, gt1, sh2, sc2, gt2 = jnp.split(mod, 6, axis=-1)

    h = rms_norm(x, norm_g[0]) * (1 + sc1) + sh1
    p = h @ w_in
    o0 = 2 * A_WIDTH
    o1 = o0 + B_QKV_W
    u = p[..., :A_WIDTH]
    v = p[..., A_WIDTH:o0]
    qkv = p[..., o0:o1].reshape(N, T, 3, N_DIL, HEADS_PER_GROUP, HEAD_DIM)
    g_a = jax.nn.sigmoid(p[..., o1:o1 + D_MODEL])
    g_b = jax.nn.sigmoid(p[..., o1 + D_MODEL:])

    vn = layer_norm(jax.nn.gelu(v), ln_v_g, ln_v_b)
    y_a = spatial_gate(u, vn, w_spatial, b_spatial)

    outs, lses, kv_rows = [], [], []
    for gi, (win, dil) in enumerate(DIL_PAIRS):
        q, k, vv = qkv[:, :, 0, gi], qkv[:, :, 1, gi], qkv[:, :, 2, gi]
        if kv_bufs is None:
            o, lse = dilated_band_attn(q, k, vv, dil, win // dil)
            keep = min(win, T)
            kv_rows.append(jnp.stack([k[:, T - keep:], vv[:, T - keep:]], axis=2))
        else:
            o, lse = dilated_cached_attn(q, k, vv, kv_bufs[gi], dil, win // dil)
            kv_rows.append(jnp.stack([k, vv], axis=2))
        outs.append(o)
        lses.append(lse)
    y_b = combine_dilations(outs, lses).reshape(N, T, B_GROUP_W)

    merged = g_a * (y_a @ w_a2d) + g_b * (y_b @ w_b2d)
    x = x + gt1 * rms_norm(merged @ w_out, norm_g[1])

    h2 = rms_norm(x, norm_g[2]) * (1 + sc2) + sh2
    f, conv_new = conv_glu(h2, w_up, conv_w, conv_b, w_down, conv_state)
    x = x + gt2 * rms_norm(f, norm_g[3])
    return x, kv_rows, conv_new, vn


def setup_inputs(seed: int = 0):
    key = jax.random.key(seed)
    ks = jax.random.split(key, 32)
    f32 = jnp.float32
    D = D_MODEL

    def nrm(k, shape, scale):
        return jax.random.normal(k, shape, f32) * scale

    swa = [nrm(ks[2 + g], (DEPTH, DEC_BATCH, min(w, PAST_LEN), 2, HEADS_PER_GROUP, HEAD_DIM), 1.0)
           for g, (w, _) in enumerate(DIL_PAIRS)]
    return {
        'x_prompt': nrm(ks[0], (BATCH, SEQ, D), 1.0),
        'x_sample': nrm(ks[1], (DEC_BATCH, DEC_SEQ, D), 1.0),
        'cache_swa0': swa[0],
        'cache_swa1': swa[1],
        'cache_swa2': swa[2],
        'state_ffn_conv': nrm(ks[5], (DEPTH, DEC_BATCH, CONV_W - 1, D_FF), 1.0),
        'c_prompt': nrm(ks[6], (BATCH, D), 1.0),
        'c_sample': nrm(ks[7], (DEC_BATCH, D), 1.0),
        'ada_w': nrm(ks[8], (DEPTH, D, 6 * D), 0.5 * D ** -0.5),
        'ada_b': nrm(ks[9], (DEPTH, 6 * D), 0.01),
        'norm_g': 1.0 + nrm(ks[10], (DEPTH, 4, D), 0.05),
        'w_in': nrm(ks[11], (DEPTH, D, IN_W), D ** -0.5),
        'ln_v_g': 1.0 + nrm(ks[12], (DEPTH, A_WIDTH), 0.05),
        'ln_v_b': nrm(ks[13], (DEPTH, A_WIDTH), 0.02),
        'w_spatial': nrm(ks[14], (DEPTH, A_GROUPS, CHUNK, CHUNK), CHUNK ** -0.5),
        'b_spatial': 1.0 + nrm(ks[15], (DEPTH, A_GROUPS, CHUNK), 0.1),
        'w_a2d': nrm(ks[16], (DEPTH, A_WIDTH, D), A_WIDTH ** -0.5),
        'w_b2d': nrm(ks[17], (DEPTH, B_GROUP_W, D), B_GROUP_W ** -0.5),
        'w_out': nrm(ks[18], (DEPTH, D, D), D ** -0.5),
        'w_up': nrm(ks[19], (DEPTH, D, 2 * D_FF), D ** -0.5),
        'conv_w': nrm(ks[20], (DEPTH, CONV_W, D_FF), CONV_W ** -0.5),
        'conv_b': nrm(ks[21], (DEPTH, D_FF), 0.02),
        'w_down': nrm(ks[22], (DEPTH, D_FF, D), D_FF ** -0.5),
    }


def reference(x_prompt, x_sample, cache_swa0, cache_swa1, cache_swa2, state_ffn_conv, c_prompt, c_sample,
              ada_w, ada_b, norm_g, w_in, ln_v_g, ln_v_b, w_spatial, b_spatial, w_a2d, w_b2d, w_out,
              w_up, conv_w, conv_b, w_down):
    y_p, y_s = x_prompt, x_sample
    swa_p = [[] for _ in range(N_DIL)]
    swa_s = [[] for _ in range(N_DIL)]
    conv_p, conv_s, chunk_v_s = [], [], []
    zero_conv = jnp.zeros((x_prompt.shape[0], CONV_W - 1, D_FF), x_prompt.dtype)
    for l in range(DEPTH):
        lw = (ada_w[l], ada_b[l], norm_g[l], w_in[l], ln_v_g[l], ln_v_b[l], w_spatial[l], b_spatial[l],
              w_a2d[l], w_b2d[l], w_out[l], w_up[l], conv_w[l], conv_b[l], w_down[l])
        y_p, kv_rows_p, cst_p, _ = trunk_layer(y_p, c_prompt, lw, None, zero_conv)
        y_s, kv_rows_s, cst_s, vn_s = trunk_layer(
            y_s, c_sample, lw, (cache_swa0[l], cache_swa1[l], cache_swa2[l]), state_ffn_conv[l])
        for g in range(N_DIL):
            swa_p[g].append(kv_rows_p[g])
            swa_s[g].append(kv_rows_s[g])
        conv_p.append(cst_p)
        conv_s.append(cst_s)
        chunk_v_s.append(vn_s)
    new_swa0_prompt = jnp.stack(swa_p[0])
    new_swa1_prompt = jnp.stack(swa_p[1])
    new_swa2_prompt = jnp.stack(swa_p[2])
    new_conv_prompt = jnp.stack(conv_p)
    new_swa0_sample = jnp.stack(swa_s[0])
    new_swa1_sample = jnp.stack(swa_s[1])
    new_swa2_sample = jnp.stack(swa_s[2])
    new_conv_sample = jnp.stack(conv_s)
    new_chunk_v_sample = jnp.stack(chunk_v_s)
    return (y_p, y_s, new_swa0_prompt, new_swa1_prompt, new_swa2_prompt, new_conv_prompt,
            new_swa0_sample, new_swa1_sample, new_swa2_sample, new_conv_sample, new_chunk_v_sample)
```

```python
import functools

import jax
import jax.numpy as jnp
from jax import lax
from jax.experimental import pallas as pl
from jax.experimental.pallas import tpu as pltpu

D = 1024
SEQ = 2048
DEPTH = 4
CHUNK = 128
A_W = 512
N_GROUPS = 3
DILS = (1, 4, 16)
WINDOWS = (128, 512, 2048)
GROUP_W = 256
HEAD_DIM = 64
QKV_W = 3 * N_GROUPS * GROUP_W
D_FF = 2816
IN_W = 2 * A_W + QKV_W + 2 * D
O_QKV = 2 * A_W
O_GATE = O_QKV + QKV_W
EPS = 1e-6
NEG = -1e30
Q_SCALE = HEAD_DIM ** -0.5

TM = 512
TM_OUT = 256
SAMPLE_SEQS_PER_STEP = 8
VMEM_LIMIT = 56 * 1024 * 1024

F32 = jnp.float32
BF16 = jnp.bfloat16


def _dot(a, b):
    return jnp.dot(a, b, preferred_element_type=F32)


def _rms(x, g):
    return x * lax.rsqrt(jnp.mean(x * x, axis=-1, keepdims=True) + EPS) * g


def _gelu(x):
    return jax.nn.gelu(x)


def _layer_norm(x, g, b):
    mu = jnp.mean(x, axis=-1, keepdims=True)
    xc = x - mu
    var = jnp.mean(xc * xc, axis=-1, keepdims=True)
    return xc * lax.rsqrt(var + EPS) * g + b


def _const_spec(shape, index_map):
    return pl.BlockSpec(shape, index_map, pipeline_mode=pl.Buffered(1))


def _params(*sem):
    return pltpu.CompilerParams(dimension_semantics=sem, vmem_limit_bytes=VMEM_LIMIT)


def _mod_kernel(c_ref, w_ref, b_ref, o_ref):
    c = c_ref[...]
    s = (c * jax.nn.sigmoid(c)).astype(BF16)
    o_ref[...] = _dot(s, w_ref[...].astype(BF16)) + b_ref[...]


def _modulation(c_all, ada_w, ada_b):
    rows = c_all.shape[0]
    tn = 1536
    return pl.pallas_call(
        _mod_kernel,
        grid=(DEPTH, 6 * D // tn),
        in_specs=[
            pl.BlockSpec((rows, D), lambda l, j: (0, 0)),
            pl.BlockSpec((None, D, tn), lambda l, j: (l, 0, j)),
            pl.BlockSpec((None, 1, tn), lambda l, j: (l, 0, j)),
        ],
        out_specs=pl.BlockSpec((None, rows, tn), lambda l, j: (l, 0, j)),
        out_shape=jax.ShapeDtypeStruct((DEPTH, rows, 6 * D), F32),
        compiler_params=_params("arbitrary", "arbitrary"),
        name="adaln_mod",
    )(c_all, ada_w, ada_b.reshape(DEPTH, 1, 6 * D))


def _project_in(x, sh1, sc1, ng_ref, win_ref, lng_ref, lnb_ref):
    h = (_rms(x, ng_ref[0:1, :]) * (1.0 + sc1) + sh1).astype(BF16)
    puv = _dot(h, win_ref[:, 0:O_QKV])
    gu = _gelu(puv[:, :A_W])
    vn = _layer_norm(_gelu(puv[:, A_W:]), lng_ref[...], lnb_ref[...])
    return h, gu, vn


def _gates_out(h, ya, win_ref, wa_ref, za_ref, gb_ref):
    pa = _dot(ya, wa_ref[...])
    pg = _dot(h, win_ref[:, O_GATE:])
    za_ref[...] = (jax.nn.sigmoid(pg[:, :D]) * pa).astype(za_ref.dtype)
    gb_ref[...] = jax.nn.sigmoid(pg[:, D:]).astype(gb_ref.dtype)


def _mix_in_prompt_kernel(x_ref, mod_ref, ng_ref, win_ref, lng_ref, lnb_ref, ws_ref, bsp_ref, wa_ref,
                          za_ref, gb_ref, qkv_ref, kv0_ref, kv1_ref, kv2_ref, ya_scr):
    tm = x_ref.shape[0]
    h, gu, vn = _project_in(x_ref[...], mod_ref[:, 0:D], mod_ref[:, D:2 * D],
                            ng_ref, win_ref, lng_ref, lnb_ref)
    vnb = vn.astype(BF16)
    ti = lax.broadcasted_iota(jnp.int32, (CHUNK, CHUNK), 0)
    si = lax.broadcasted_iota(jnp.int32, (CHUNK, CHUNK), 1)
    causal = si <= ti
    for g in range(A_W // CHUNK):
        wg = jnp.where(causal, ws_ref[g], 0.0).astype(BF16)
        for c in range(tm // CHUNK):
            rows = slice(c * CHUNK, (c + 1) * CHUNK)
            cols = slice(g * CHUNK, (g + 1) * CHUNK)
            mix = _dot(wg, vnb[rows, cols]) + bsp_ref[:, cols]
            ya_scr[rows, cols] = (gu[rows, cols] * mix).astype(BF16)
    _gates_out(h, ya_scr[...], win_ref, wa_ref, za_ref, gb_ref)
    pq = _dot(h, win_ref[:, O_QKV:O_GATE])
    for g in range(N_GROUPS):
        o = g * 3 * GROUP_W
        qkv_ref[:, o:o + GROUP_W] = (pq[:, o:o + GROUP_W] * Q_SCALE).astype(BF16)
        qkv_ref[:, o + GROUP_W:o + 3 * GROUP_W] = pq[:, o + GROUP_W:o + 3 * GROUP_W].astype(BF16)
    kv0_ref[...] = pq[tm - kv0_ref.shape[0]:, GROUP_W:3 * GROUP_W]
    kv1_ref[...] = pq[tm - kv1_ref.shape[0]:, 4 * GROUP_W:6 * GROUP_W]
    kv2_ref[...] = pq[tm - kv2_ref.shape[0]:, 7 * GROUP_W:9 * GROUP_W]


def _mix_in_prompt(l, x, mod, norm_g, w_in, ln_g, ln_b, w_sp, b_sp, w_a2d):
    n, t, _ = x.shape
    steps = t // TM
    keep = [min(w, t) for w in WINDOWS]
    kv_rows = [min(k, TM) for k in keep]
    first = [(t - k) // TM for k in keep]

    def kv_spec(g):
        return pl.BlockSpec((None, kv_rows[g], 2 * GROUP_W),
                            lambda i, j, g=g: (i, jnp.maximum(j - first[g], 0), 0))

    tok = lambda w: pl.BlockSpec((None, TM, w), lambda i, j: (i, j, 0))
    return pl.pallas_call(
        _mix_in_prompt_kernel,
        grid=(n, steps),
        in_specs=[
            tok(D),
            pl.BlockSpec((None, 1, 6 * D), lambda i, j: (i, 0, 0)),
            _const_spec((None, 4, D), lambda i, j: (l, 0, 0)),
            _const_spec((None, D, IN_W), lambda i, j: (l, 0, 0)),
            _const_spec((None, 1, A_W), lambda i, j: (l, 0, 0)),
            _const_spec((None, 1, A_W), lambda i, j: (l, 0, 0)),
            _const_spec((None, 4, CHUNK, CHUNK), lambda i, j: (l, 0, 0, 0)),
            _const_spec((None, CHUNK, A_W), lambda i, j: (l, 0, 0)),
            _const_spec((None, A_W, D), lambda i, j: (l, 0, 0)),
        ],
        out_specs=[tok(D), tok(D), tok(QKV_W), kv_spec(0), kv_spec(1), kv_spec(2)],
        out_shape=[
            jax.ShapeDtypeStruct((n, t, D), BF16),
            jax.ShapeDtypeStruct((n, t, D), BF16),
            jax.ShapeDtypeStruct((n, t, QKV_W), BF16),
            jax.ShapeDtypeStruct((n, keep[0], 2 * GROUP_W), F32),
            jax.ShapeDtypeStruct((n, keep[1], 2 * GROUP_W), F32),
            jax.ShapeDtypeStruct((n, keep[2], 2 * GROUP_W), F32),
        ],
        scratch_shapes=[pltpu.VMEM((TM, A_W), BF16)],
        compiler_params=_params("arbitrary", "arbitrary"),
        name="mix_in_prompt",
    )(x, mod, norm_g, w_in, ln_g, ln_b, w_sp, b_sp, w_a2d)


def _mix_in_sample_kernel(x_ref, mod_ref, ng_ref, win_ref, lng_ref, lnb_ref, csp_ref, bsp_ref, wa_ref,
                          za_ref, gb_ref, qkv_ref, vn_ref, *, steps, seqs):
    sh1 = jnp.concatenate([mod_ref[:, 0:D]] * steps, axis=0)
    sc1 = jnp.concatenate([mod_ref[:, D:2 * D]] * steps, axis=0)
    h, gu, vn = _project_in(x_ref[...], sh1, sc1, ng_ref, win_ref, lng_ref, lnb_ref)
    vn_ref[...] = vn
    ya = []
    for t in range(steps):
        mix = bsp_ref[t:t + 1, :]
        for s in range(t + 1):
            mix = mix + csp_ref[t * steps + s:t * steps + s + 1, :] * vn[s * seqs:(s + 1) * seqs, :]
        ya.append(gu[t * seqs:(t + 1) * seqs, :] * mix)
    ya = jnp.concatenate(ya, axis=0).astype(BF16)
    _gates_out(h, ya, win_ref, wa_ref, za_ref, gb_ref)
    qkv_ref[...] = _dot(h, win_ref[:, O_QKV:O_GATE])


def _mix_in_sample(l, x, mod, norm_g, w_in, ln_g, ln_b, c_sp, b_sp, w_a2d, steps, seqs):
    rows = steps * seqs
    full = lambda w: pl.BlockSpec((rows, w), lambda i: (0, 0))
    return pl.pallas_call(
        functools.partial(_mix_in_sample_kernel, steps=steps, seqs=seqs),
        grid=(1,),
        in_specs=[
            full(D),
            pl.BlockSpec((seqs, 6 * D), lambda i: (0, 0)),
            _const_spec((None, 4, D), lambda i: (l, 0, 0)),
            _const_spec((None, D, IN_W), lambda i: (l, 0, 0)),
            _const_spec((None, 1, A_W), lambda i: (l, 0, 0)),
            _const_spec((None, 1, A_W), lambda i: (l, 0, 0)),
            _const_spec((None, steps * steps, A_W), lambda i: (l, 0, 0)),
            _const_spec((None, 8, A_W), lambda i: (l, 0, 0)),
            _const_spec((None, A_W, D), lambda i: (l, 0, 0)),
        ],
        out_specs=[full(D), full(D), full(QKV_W), full(A_W)],
        out_shape=[
            jax.ShapeDtypeStruct((rows, D), BF16),
            jax.ShapeDtypeStruct((rows, D), BF16),
            jax.ShapeDtypeStruct((rows, QKV_W), F32),
            jax.ShapeDtypeStruct((rows, A_W), F32),
        ],
        compiler_params=_params("arbitrary"),
        name="mix_in_sample",
    )(x, mod, norm_g, w_in, ln_g, ln_b, c_sp, b_sp, w_a2d)


def _band_attn_kernel(q_ref, k_ref, v_ref, o_ref, lse_ref, *, span):
    length = q_ref.shape[0]
    nb = length // CHUNK
    kw = CHUNK if nb == 1 else 2 * CHUNK
    head = lax.broadcasted_iota(jnp.int32, (1, GROUP_W), 1) // HEAD_DIM
    qi = lax.broadcasted_iota(jnp.int32, (CHUNK, kw), 0)
    ki = lax.broadcasted_iota(jnp.int32, (CHUNK, kw), 1)

    def block(b, carry):
        q0 = pl.multiple_of(b * CHUNK, CHUNK)
        k0 = pl.multiple_of(jnp.maximum(b - 1, 0) * CHUNK, CHUNK)
        q = q_ref[pl.ds(q0, CHUNK), :]
        k = k_ref[pl.ds(k0, kw), :]
        v = v_ref[pl.ds(k0, kw), :]
        dist = (q0 - k0) + qi - ki
        valid = (dist >= 0) & (dist <= span)
        acc = jnp.zeros((CHUNK, GROUP_W), F32)
        lse = jnp.zeros((CHUNK, GROUP_W), F32)
        for hh in range(GROUP_W // HEAD_DIM):
            hm = head == hh
            qm = jnp.where(hm, q, jnp.zeros_like(q))
            s = lax.dot_general(qm, k, (((1,), (1,)), ((), ())), preferred_element_type=F32)
            s = jnp.where(valid, s, NEG)
            m = jnp.max(s, axis=-1, keepdims=True)
            p = jnp.exp(s - m)
            z = jnp.sum(p, axis=-1, keepdims=True)
            oh = _dot(p.astype(BF16), v)
            acc = jnp.where(hm, oh / z, acc)
            lse = jnp.where(hm, m + jnp.log(z), lse)
        o_ref[pl.ds(q0, CHUNK), :] = acc.astype(o_ref.dtype)
        lse_ref[pl.ds(q0, CHUNK), :] = lse
        return carry

    lax.fori_loop(0, nb, block, 0)


def _band_attn(qkv, g):
    n, t, _ = qkv.shape
    dil = DILS[g]
    length = t // dil
    span = WINDOWS[g] // dil
    view = qkv.reshape(n, length, dil * QKV_W)
    blocks_per_row = QKV_W // GROUP_W

    def in_spec(which):
        return pl.BlockSpec((None, length, GROUP_W),
                            lambda i, r: (i, 0, r * blocks_per_row + 3 * g + which))

    out_spec = pl.BlockSpec((None, length, GROUP_W), lambda i, r: (i, 0, r))
    o, lse = pl.pallas_call(
        functools.partial(_band_attn_kernel, span=span),
        grid=(n, dil),
        in_specs=[in_spec(0), in_spec(1), in_spec(2)],
        out_specs=[out_spec, out_spec],
        out_shape=[
            jax.ShapeDtypeStruct((n, length, dil * GROUP_W), BF16),
            jax.ShapeDtypeStruct((n, length, dil * GROUP_W), F32),
        ],
        compiler_params=_params("arbitrary", "arbitrary"),
        name=f"band_attn_g{g}",
    )(view, view, view)
    return o.reshape(n, t, GROUP_W), lse.reshape(n, t, GROUP_W)


def _cache_attn_kernel(q_ref, kn_ref, vn_ref, c_ref, o_ref, lse_ref, m_scr, z_scr, a_scr,
                       *, shared_keys):
    steps, nseq, _ = q_ref.shape
    rows = c_ref.shape[1]
    li = lax.broadcasted_iota(jnp.int32, (GROUP_W, GROUP_W), 0) // HEAD_DIM
    lj = lax.broadcasted_iota(jnp.int32, (GROUP_W, GROUP_W), 1) // HEAD_DIM
    head_sum = (li == lj).astype(BF16)
    ri = lax.broadcasted_iota(jnp.int32, (rows, GROUP_W), 0)

    def lane0(t):
        return 0 if shared_keys else t * 2 * GROUP_W

    def one_seq(i, carry):
        prods = []
        for t in range(steps):
            kt = c_ref[i, :, lane0(t):lane0(t) + GROUP_W]
            qt = q_ref[t, pl.ds(i, 1), :] * Q_SCALE
            prods.append((kt * qt).astype(BF16))
        s_all = _dot(jnp.concatenate(prods, axis=0), head_sum)
        for t in range(steps):
            s = s_all[t * rows:(t + 1) * rows, :]
            if shared_keys:
                s = jnp.where(ri >= t, s, NEG)
            m = jnp.max(s, axis=0, keepdims=True)
            p = jnp.exp(s - m)
            vt = c_ref[i, :, lane0(t) + GROUP_W:lane0(t) + 2 * GROUP_W]
            m_scr[t, pl.ds(i, 1), :] = m
            z_scr[t, pl.ds(i, 1), :] = jnp.sum(p, axis=0, keepdims=True)
            a_scr[t, pl.ds(i, 1), :] = jnp.sum(p * vt, axis=0, keepdims=True)
        return carry

    lax.fori_loop(0, nseq, one_seq, 0)

    for t in range(steps):
        new = list(range(t + 1)) if shared_keys else [t]
        qt = q_ref[t] * Q_SCALE
        s_new = [_dot((qt * kn_ref[u]).astype(BF16), head_sum) for u in new]
        m_c = m_scr[t]
        m = m_c
        for s in s_new:
            m = jnp.maximum(m, s)
        w_c = jnp.exp(m_c - m)
        z = z_scr[t] * w_c
        a = a_scr[t] * w_c
        for s, u in zip(s_new, new):
            w = jnp.exp(s - m)
            z = z + w
            a = a + w * vn_ref[u]
        o_ref[t] = a / z
        lse_ref[t] = m + jnp.log(z)


def _cache_attn(l, qkv, cache, g, steps, seqs):
    dil = DILS[g]
    wb = cache.shape[2]
    rows = wb // dil if dil > 1 else wb
    cache_v = cache.reshape(DEPTH, seqs, rows, (wb // rows) * 2 * GROUP_W)
    lanes = 2 * GROUP_W if dil == 1 else steps * 2 * GROUP_W
    nb = SAMPLE_SEQS_PER_STEP
    qkv3 = qkv.reshape(steps, seqs, QKV_W)
    new_spec = lambda which: pl.BlockSpec((steps, nb, GROUP_W), lambda i: (0, i, 3 * g + which))
    out_spec = pl.BlockSpec((steps, nb, GROUP_W), lambda i: (0, i, 0))
    return pl.pallas_call(
        functools.partial(_cache_attn_kernel, shared_keys=(dil == 1)),
        grid=(seqs // nb,),
        in_specs=[new_spec(0), new_spec(1), new_spec(2),
                  pl.BlockSpec((None, nb, rows, lanes), lambda i: (l, i, 0, 0))],
        out_specs=[out_spec, out_spec],
        out_shape=[jax.ShapeDtypeStruct((steps, seqs, GROUP_W), F32)] * 2,
        scratch_shapes=[pltpu.VMEM((steps, nb, GROUP_W), F32)] * 3,
        compiler_params=_params("arbitrary"),
        name=f"cache_attn_g{g}",
    )(qkv3, qkv3, qkv3, cache_v)


def _merge_groups(o_refs, lse_refs):
    lses = [r[...] for r in lse_refs]
    lmax = jnp.maximum(jnp.maximum(lses[0], lses[1]), lses[2])
    num = 0.0
    den = 0.0
    for o_ref, ls in zip(o_refs, lses):
        e = jnp.exp(ls - lmax)
        num = num + e * o_ref[...].astype(F32)
        den = den + e
    return (num / den).astype(BF16)


def _mix_out_front(x, yb, za_ref, gb_ref, mods, ng_ref, wb_ref, wo_ref, wup_ref):
    sh2, sc2, gt1 = mods
    merged = za_ref[...].astype(F32) + gb_ref[...].astype(F32) * _dot(yb, wb_ref[...])
    x1 = x + gt1 * _rms(_dot(merged.astype(BF16), wo_ref[...]), ng_ref[1:2, :])
    h2 = (_rms(x1, ng_ref[2:3, :]) * (1.0 + sc2) + sh2).astype(BF16)
    return x1, _dot(h2, wup_ref[...])


def _mix_out_back(x1, conv, val, gt2, ng_ref, wdn_ref, y_ref):
    f = _dot((_gelu(conv) * val).astype(BF16), wdn_ref[...])
    y_ref[...] = x1 + gt2 * _rms(f, ng_ref[3:4, :])


def _mix_out_prompt_kernel(x_ref, za_ref, gb_ref, o0_ref, o1_ref, o2_ref, l0_ref, l1_ref, l2_ref,
                           mod_ref, ng_ref, wb_ref, wo_ref, wup_ref, cw_ref, cb_ref, wdn_ref,
                           y_ref, cs_ref, tail_scr):
    tm = x_ref.shape[0]

    @pl.when(pl.program_id(1) == 0)
    def _():
        tail_scr[...] = jnp.zeros_like(tail_scr)

    yb = _merge_groups((o0_ref, o1_ref, o2_ref), (l0_ref, l1_ref, l2_ref))
    mods = (mod_ref[:, 3 * D:4 * D], mod_ref[:, 4 * D:5 * D], mod_ref[:, 2 * D:3 * D])
    x1, up = _mix_out_front(x_ref[...], yb, za_ref, gb_ref, mods, ng_ref, wb_ref, wo_ref, wup_ref)
    gate = up[:, :D_FF]
    prev = tail_scr[...]
    row = lax.broadcasted_iota(jnp.int32, (tm, D_FF), 0)
    g1 = jnp.where(row == 0, prev[7:8, :], pltpu.roll(gate, 1, 0))
    g2 = jnp.where(row == 0, prev[6:7, :], jnp.where(row == 1, prev[7:8, :], pltpu.roll(gate, 2, 0)))
    conv = cb_ref[...] + cw_ref[0:1, :] * g2 + cw_ref[1:2, :] * g1 + cw_ref[2:3, :] * gate
    tail_scr[...] = gate[tm - 8:, :]
    cs_ref[...] = gate[tm - 2:, :]
    _mix_out_back(x1, conv, up[:, D_FF:], mod_ref[:, 5 * D:6 * D], ng_ref, wdn_ref, y_ref)


def _mix_out_prompt(l, x, za, gb, os, lses, mod, norm_g, w_b2d, w_out, w_up, conv_w, conv_b, w_down):
    n, t, _ = x.shape
    tm = TM_OUT
    tok = lambda w: pl.BlockSpec((None, tm, w), lambda i, j: (i, j, 0))
    return pl.pallas_call(
        _mix_out_prompt_kernel,
        grid=(n, t // tm),
        in_specs=[
            tok(D), tok(D), tok(D),
            tok(GROUP_W), tok(GROUP_W), tok(GROUP_W), tok(GROUP_W), tok(GROUP_W), tok(GROUP_W),
            pl.BlockSpec((None, 1, 6 * D), lambda i, j: (i, 0, 0)),
            _const_spec((None, 4, D), lambda i, j: (l, 0, 0)),
            _const_spec((None, GROUP_W, D), lambda i, j: (l, 0, 0)),
            _const_spec((None, D, D), lambda i, j: (l, 0, 0)),
            _const_spec((None, D, 2 * D_FF), lambda i, j: (l, 0, 0)),
            _const_spec((None, 3, D_FF), lambda i, j: (l, 0, 0)),
            _const_spec((None, 1, D_FF), lambda i, j: (l, 0, 0)),
            _const_spec((None, D_FF, D), lambda i, j: (l, 0, 0)),
        ],
        out_specs=[tok(D), pl.BlockSpec((None, 2, D_FF), lambda i, j: (i, 0, 0))],
        out_shape=[
            jax.ShapeDtypeStruct((n, t, D), F32),
            jax.ShapeDtypeStruct((n, 2, D_FF), F32),
        ],
        scratch_shapes=[pltpu.VMEM((8, D_FF), F32)],
        compiler_params=_params("arbitrary", "arbitrary"),
        name="mix_out_prompt",
    )(x, za, gb, *os, *lses, mod, norm_g, w_b2d, w_out, w_up, conv_w, conv_b, w_down)


def _mix_out_sample_kernel(x_ref, za_ref, gb_ref, o0_ref, o1_ref, o2_ref, l0_ref, l1_ref, l2_ref,
                           mod_ref, st_ref, ng_ref, wb_ref, wo_ref, wup_ref, cw_ref, cb_ref, wdn_ref,
                           y_ref, cs_ref, *, steps, seqs):
    rep = lambda a: jnp.concatenate([a] * steps, axis=0)
    yb = _merge_groups((o0_ref, o1_ref, o2_ref), (l0_ref, l1_ref, l2_ref))
    mods = (rep(mod_ref[:, 3 * D:4 * D]), rep(mod_ref[:, 4 * D:5 * D]), rep(mod_ref[:, 2 * D:3 * D]))
    x1, up = _mix_out_front(x_ref[...], yb, za_ref, gb_ref, mods, ng_ref, wb_ref, wo_ref, wup_ref)
    gate = up[:, :D_FF]
    hist = jnp.concatenate([st_ref[0], st_ref[1], gate], axis=0)
    rows = steps * seqs
    conv = (cb_ref[...] + cw_ref[0:1, :] * hist[0:rows, :] + cw_ref[1:2, :] * hist[seqs:seqs + rows, :]
            + cw_ref[2:3, :] * gate)
    cs_ref[0] = hist[rows:rows + seqs, :]
    cs_ref[1] = hist[rows + seqs:rows + 2 * seqs, :]
    _mix_out_back(x1, conv, up[:, D_FF:], rep(mod_ref[:, 5 * D:6 * D]), ng_ref, wdn_ref, y_ref)


def _mix_out_sample(l, x, za, gb, os, lses, mod, state, norm_g, w_b2d, w_out, w_up, conv_w, conv_b,
                    w_down, steps, seqs):
    rows = steps * seqs
    full = lambda w: pl.BlockSpec((rows, w), lambda i: (0, 0))
    return pl.pallas_call(
        functools.partial(_mix_out_sample_kernel, steps=steps, seqs=seqs),
        grid=(1,),
        in_specs=[
            full(D), full(D), full(D),
            full(GROUP_W), full(GROUP_W), full(GROUP_W), full(GROUP_W), full(GROUP_W), full(GROUP_W),
            pl.BlockSpec((seqs, 6 * D), lambda i: (0, 0)),
            pl.BlockSpec((2, seqs, D_FF), lambda i: (0, 0, 0)),
            _const_spec((None, 4, D), lambda i: (l, 0, 0)),
            _const_spec((None, GROUP_W, D), lambda i: (l, 0, 0)),
            _const_spec((None, D, D), lambda i: (l, 0, 0)),
            _const_spec((None, D, 2 * D_FF), lambda i: (l, 0, 0)),
            _const_spec((None, 3, D_FF), lambda i: (l, 0, 0)),
            _const_spec((None, 1, D_FF), lambda i: (l, 0, 0)),
            _const_spec((None, D_FF, D), lambda i: (l, 0, 0)),
        ],
        out_specs=[full(D), pl.BlockSpec((2, seqs, D_FF), lambda i: (0, 0, 0))],
        out_shape=[
            jax.ShapeDtypeStruct((rows, D), F32),
            jax.ShapeDtypeStruct((2, seqs, D_FF), F32),
        ],
        compiler_params=_params("arbitrary"),
        name="mix_out_sample",
    )(x, za, gb, *os, *lses, mod, state, norm_g, w_b2d, w_out, w_up, conv_w, conv_b, w_down)


def _group_major_qkv(w_in):
    qkv = w_in[:, :, O_QKV:O_GATE].reshape(DEPTH, D, 3, N_GROUPS, GROUP_W)
    qkv = qkv.transpose(0, 1, 3, 2, 4).reshape(DEPTH, D, QKV_W)
    return jnp.concatenate([w_in[:, :, :O_QKV], qkv, w_in[:, :, O_GATE:]], axis=-1)


def kernel(x_prompt, x_sample, cache_swa0, cache_swa1, cache_swa2, state_ffn_conv, c_prompt, c_sample,
           ada_w, ada_b, norm_g, w_in, ln_v_g, ln_v_b, w_spatial, b_spatial, w_a2d, w_b2d, w_out,
           w_up, conv_w, conv_b, w_down):
    n_p, t_p, _ = x_prompt.shape
    seqs, steps, _ = x_sample.shape
    caches = (cache_swa0, cache_swa1, cache_swa2)

    w_in_b = _group_major_qkv(w_in).astype(BF16)
    w_a2d_b, w_b2d_b, w_out_b = w_a2d.astype(BF16), w_b2d.astype(BF16), w_out.astype(BF16)
    w_up_b, w_down_b = w_up.astype(BF16), w_down.astype(BF16)
    ln_g3 = ln_v_g.reshape(DEPTH, 1, A_W)
    ln_b3 = ln_v_b.reshape(DEPTH, 1, A_W)
    conv_b3 = conv_b.reshape(DEPTH, 1, D_FF)
    b_sp = jnp.repeat(b_spatial.transpose(0, 2, 1), CHUNK, axis=-1)
    c_sp = jnp.repeat(w_spatial[:, :, :steps, :steps].transpose(0, 2, 3, 1), CHUNK, axis=-1)
    c_sp = c_sp.reshape(DEPTH, steps * steps, A_W)

    mod = _modulation(jnp.concatenate([c_prompt, c_sample], axis=0), ada_w, ada_b)
    mod_p = mod[:, :n_p].reshape(DEPTH, n_p, 1, 6 * D)
    mod_s = mod[:, n_p:]

    y_p = x_prompt
    y_s = x_sample.transpose(1, 0, 2).reshape(steps * seqs, D)
    state_s = state_ffn_conv.transpose(0, 2, 1, 3)

    swa_p = [[] for _ in range(N_GROUPS)]
    swa_s = [[] for _ in range(N_GROUPS)]
    conv_p, conv_s, chunk_v_s = [], [], []
    for l in range(DEPTH):
        za, gb, qkv, kv0, kv1, kv2 = _mix_in_prompt(
            l, y_p, mod_p[l], norm_g, w_in_b, ln_g3, ln_b3, w_spatial, b_sp, w_a2d_b)
        attn = [_band_attn(qkv, g) for g in range(N_GROUPS)]
        y_p, cst_p = _mix_out_prompt(
            l, y_p, za, gb, [a[0] for a in attn], [a[1] for a in attn], mod_p[l], norm_g,
            w_b2d_b, w_out_b, w_up_b, conv_w, conv_b3, w_down_b)
        for g, kv in enumerate((kv0, kv1, kv2)):
            swa_p[g].append(kv.reshape(n_p, kv.shape[1], 2, GROUP_W // HEAD_DIM, HEAD_DIM))
        conv_p.append(cst_p)

        za_s, gb_s, qkv_s, vn_s = _mix_in_sample(
            l, y_s, mod_s[l], norm_g, w_in_b, ln_g3, ln_b3, c_sp, b_sp, w_a2d_b, steps, seqs)
        attn_s = [_cache_attn(l, qkv_s, caches[g], g, steps, seqs) for g in range(N_GROUPS)]
        flat = lambda a: a.reshape(steps * seqs, GROUP_W)
        y_s, cst_s = _mix_out_sample(
            l, y_s, za_s, gb_s, [flat(a[0]) for a in attn_s], [flat(a[1]) for a in attn_s],
            mod_s[l], state_s[l], norm_g, w_b2d_b, w_out_b, w_up_b, conv_w, conv_b3, w_down_b,
            steps, seqs)
        qkv_s3 = qkv_s.reshape(steps, seqs, N_GROUPS, 3, GROUP_W // HEAD_DIM, HEAD_DIM)
        for g in range(N_GROUPS):
            swa_s[g].append(qkv_s3[:, :, g, 1:3].transpose(1, 0, 2, 3, 4))
        conv_s.append(cst_s.transpose(1, 0, 2))
        chunk_v_s.append(vn_s.reshape(steps, seqs, A_W).transpose(1, 0, 2))

    y_s = y_s.reshape(steps, seqs, D).transpose(1, 0, 2)
    return (y_p, y_s,
            jnp.stack(swa_p[0]), jnp.stack(swa_p[1]), jnp.stack(swa_p[2]), jnp.stack(conv_p),
            jnp.stack(swa_s[0]), jnp.stack(swa_s[1]), jnp.stack(swa_s[2]), jnp.stack(conv_s),
            jnp.stack(chunk_v_s))
```

```python
import functools

import jax
import jax.numpy as jnp
from jax import lax
from jax.experimental import pallas as pl
from jax.experimental.pallas import tpu as pltpu

D = 1024
DEPTH = 4
CHUNK = 128
LANES = 128
A_W = 512
N_GROUPS = 3
DILS = (1, 4, 16)
WINDOWS = (128, 512, 2048)
GROUP_W = 256
HEAD_DIM = 64
HEADS = GROUP_W // HEAD_DIM
QKV_G = 3 * GROUP_W
QKV_W = N_GROUPS * QKV_G
D_FF = 2816
IN_W = 2 * A_W + QKV_W + 2 * D
O_QKV = 2 * A_W
O_GATE = O_QKV + QKV_W
CONV_W = 3
EPS = 1e-6
NEG = -1e30
Q_SCALE = HEAD_DIM ** -0.5

TM = 512
TM_OUT = 256
TM_SAMPLE = 128
SAMPLE_SEQS_PER_STEP = 2
VMEM_LIMIT = 56 * 1024 * 1024

F32 = jnp.float32
BF16 = jnp.bfloat16
_NT = (((1,), (1,)), ((), ()))


def _dot(a, b):
    return jnp.dot(a, b, preferred_element_type=F32)


def _dot_nt(a, b):
    return lax.dot_general(a, b, _NT, preferred_element_type=F32)


def _rms(x, g):
    return x * lax.rsqrt(jnp.mean(x * x, axis=-1, keepdims=True) + EPS) * g


def _gelu(x):
    return jax.nn.gelu(x)


def _layer_norm(x, g, b):
    mu = jnp.mean(x, axis=-1, keepdims=True)
    xc = x - mu
    var = jnp.mean(xc * xc, axis=-1, keepdims=True)
    return xc * lax.rsqrt(var + EPS) * g + b


def _const_spec(shape, index_map):
    return pl.BlockSpec(shape, index_map, pipeline_mode=pl.Buffered(1))


def _params(*sem):
    return pltpu.CompilerParams(dimension_semantics=sem, vmem_limit_bytes=VMEM_LIMIT)


def _head_of_lane():
    return lax.broadcasted_iota(jnp.int32, (1, GROUP_W), 1) // HEAD_DIM


def _mod_kernel(c_ref, w_ref, b_ref, o_ref):
    c = c_ref[...]
    s = (c * jax.nn.sigmoid(c)).astype(BF16)
    o_ref[...] = _dot(s, w_ref[...].astype(BF16)) + b_ref[...]


def _modulation(c_all, ada_w, ada_b):
    rows = c_all.shape[0]
    tn = 1536
    return pl.pallas_call(
        _mod_kernel,
        grid=(DEPTH, 6 * D // tn),
        in_specs=[
            pl.BlockSpec((rows, D), lambda l, j: (0, 0)),
            pl.BlockSpec((None, D, tn), lambda l, j: (l, 0, j)),
            pl.BlockSpec((None, 1, tn), lambda l, j: (l, 0, j)),
        ],
        out_specs=pl.BlockSpec((None, rows, tn), lambda l, j: (l, 0, j)),
        out_shape=jax.ShapeDtypeStruct((DEPTH, rows, 6 * D), F32),
        compiler_params=_params("arbitrary", "arbitrary"),
        name="adaln_mod",
    )(c_all, ada_w, ada_b.reshape(DEPTH, 1, 6 * D))


def _project_in(x, sh1, sc1, ng_ref, win_ref, lng_ref, lnb_ref):
    h = (_rms(x, ng_ref[0:1, :]) * (1.0 + sc1) + sh1).astype(BF16)
    puv = _dot(h, win_ref[:, 0:O_QKV])
    gu = _gelu(puv[:, :A_W])
    vn = _layer_norm(_gelu(puv[:, A_W:]), lng_ref[...], lnb_ref[...])
    return h, gu, vn


def _gates_out(h, ya, win_ref, wa_ref, za_ref, gb_ref):
    pa = _dot(ya, wa_ref[...])
    pg = _dot(h, win_ref[:, O_GATE:])
    za_ref[...] = (jax.nn.sigmoid(pg[:, :D]) * pa).astype(za_ref.dtype)
    gb_ref[...] = jax.nn.sigmoid(pg[:, D:]).astype(gb_ref.dtype)


def _mix_in_prompt_kernel(x_ref, mod_ref, ng_ref, win_ref, lng_ref, lnb_ref, ws_ref, bsp_ref, wa_ref,
                          za_ref, gb_ref, qkv0_ref, qkv1_ref, qkv2_ref, kv0_ref, kv1_ref, kv2_ref,
                          ya_scr, perm_scr):
    tm = x_ref.shape[0]
    h, gu, vn = _project_in(x_ref[...], mod_ref[:, 0:D], mod_ref[:, D:2 * D],
                            ng_ref, win_ref, lng_ref, lnb_ref)
    vnb = vn.astype(BF16)
    ti = lax.broadcasted_iota(jnp.int32, (CHUNK, CHUNK), 0)
    si = lax.broadcasted_iota(jnp.int32, (CHUNK, CHUNK), 1)
    causal = si <= ti
    for g in range(A_W // CHUNK):
        wg = jnp.where(causal, ws_ref[g], 0.0).astype(BF16)
        for c in range(tm // CHUNK):
            rows = slice(c * CHUNK, (c + 1) * CHUNK)
            cols = slice(g * CHUNK, (g + 1) * CHUNK)
            mix = _dot(wg, vnb[rows, cols]) + bsp_ref[:, cols]
            ya_scr[rows, cols] = (gu[rows, cols] * mix).astype(BF16)
    _gates_out(h, ya_scr[...], win_ref, wa_ref, za_ref, gb_ref)

    pq = _dot(h, win_ref[:, O_QKV:O_GATE])
    kv0_ref[...] = pq[tm - kv0_ref.shape[0]:, GROUP_W:QKV_G]
    kv1_ref[...] = pq[tm - kv1_ref.shape[0]:, QKV_G + GROUP_W:2 * QKV_G]
    kv2_ref[...] = pq[tm - kv2_ref.shape[0]:, 2 * QKV_G + GROUP_W:3 * QKV_G]
    qkv0_ref[0, :, 0:GROUP_W] = (pq[:, 0:GROUP_W] * Q_SCALE).astype(BF16)
    qkv0_ref[0, :, GROUP_W:QKV_G] = pq[:, GROUP_W:QKV_G].astype(BF16)
    slabs = QKV_G // LANES
    for g, out_ref in ((1, qkv1_ref), (2, qkv2_ref)):
        dil = DILS[g]
        for s in range(slabs):
            col = g * QKV_G + s * LANES
            scale = Q_SCALE if s * LANES < GROUP_W else 1.0
            perm_scr[s] = pq[:, col:col + LANES] * scale
        for r in range(dil):
            for s in range(slabs):
                out_ref[r, :, s * LANES:(s + 1) * LANES] = (
                    perm_scr[s, pl.ds(r, tm // dil, stride=dil), :].astype(BF16))


def _mix_in_prompt(l, x, mod, norm_g, w_in, ln_g, ln_b, w_sp, b_sp, w_a2d):
    n, t, _ = x.shape
    steps = t // TM
    keep = [min(w, t) for w in WINDOWS]
    kv_rows = [min(k, TM) for k in keep]
    first = [(t - k) // TM for k in keep]

    def kv_spec(g):
        return pl.BlockSpec((None, kv_rows[g], 2 * GROUP_W),
                            lambda i, j, g=g: (i, jnp.maximum(j - first[g], 0), 0))

    def qkv_spec(g):
        return pl.BlockSpec((None, DILS[g], TM // DILS[g], QKV_G), lambda i, j: (i, 0, j, 0))

    tok = lambda w: pl.BlockSpec((None, TM, w), lambda i, j: (i, j, 0))
    return pl.pallas_call(
        _mix_in_prompt_kernel,
        grid=(n, steps),
        in_specs=[
            tok(D),
            pl.BlockSpec((None, 1, 6 * D), lambda i, j: (i, 0, 0)),
            _const_spec((None, 4, D), lambda i, j: (l, 0, 0)),
            _const_spec((None, D, IN_W), lambda i, j: (l, 0, 0)),
            _const_spec((None, 1, A_W), lambda i, j: (l, 0, 0)),
            _const_spec((None, 1, A_W), lambda i, j: (l, 0, 0)),
            _const_spec((None, 4, CHUNK, CHUNK), lambda i, j: (l, 0, 0, 0)),
            _const_spec((None, CHUNK, A_W), lambda i, j: (l, 0, 0)),
            _const_spec((None, A_W, D), lambda i, j: (l, 0, 0)),
        ],
        out_specs=[tok(D), tok(D), qkv_spec(0), qkv_spec(1), qkv_spec(2),
                   kv_spec(0), kv_spec(1), kv_spec(2)],
        out_shape=[
            jax.ShapeDtypeStruct((n, t, D), BF16),
            jax.ShapeDtypeStruct((n, t, D), BF16),
        ] + [jax.ShapeDtypeStruct((n, d, t // d, QKV_G), BF16) for d in DILS] + [
            jax.ShapeDtypeStruct((n, k, 2 * GROUP_W), F32) for k in keep
        ],
        scratch_shapes=[pltpu.VMEM((TM, A_W), BF16), pltpu.VMEM((QKV_G // LANES, TM, LANES), F32)],
        compiler_params=_params("arbitrary", "arbitrary"),
        name="mix_in_prompt",
    )(x, mod, norm_g, w_in, ln_g, ln_b, w_sp, b_sp, w_a2d)


def _mix_in_sample_kernel(x_ref, mod_ref, ng_ref, win_ref, lng_ref, lnb_ref, csp_ref, bsp_ref, wa_ref,
                          za_ref, gb_ref, qkv_ref, vn_ref, *, steps):
    tm = x_ref.shape[0]
    h, gu, vn = _project_in(x_ref[...], mod_ref[:, 0:D], mod_ref[:, D:2 * D],
                            ng_ref, win_ref, lng_ref, lnb_ref)
    vn_ref[...] = vn
    mix = jnp.zeros((tm // 8, 8, A_W), F32) + bsp_ref[...][None]
    for k in range(steps):
        prev = vn if k == 0 else pltpu.roll(vn, k, 0)
        mix = mix + csp_ref[k][None] * prev.reshape(tm // 8, 8, A_W)
    ya = (gu * mix.reshape(tm, A_W)).astype(BF16)
    _gates_out(h, ya, win_ref, wa_ref, za_ref, gb_ref)
    qkv_ref[...] = _dot(h, win_ref[:, O_QKV:O_GATE])


def _mix_in_sample(l, x, mod, norm_g, w_in, ln_g, ln_b, c_sp, b_sp, w_a2d, steps):
    rows = x.shape[0]
    tm = TM_SAMPLE
    tok = lambda w: pl.BlockSpec((tm, w), lambda i: (i, 0))
    return pl.pallas_call(
        functools.partial(_mix_in_sample_kernel, steps=steps),
        grid=(rows // tm,),
        in_specs=[
            tok(D),
            pl.BlockSpec((tm, 2 * D), lambda i: (i, 0)),
            _const_spec((None, 4, D), lambda i: (l, 0, 0)),
            _const_spec((None, D, IN_W), lambda i: (l, 0, 0)),
            _const_spec((None, 1, A_W), lambda i: (l, 0, 0)),
            _const_spec((None, 1, A_W), lambda i: (l, 0, 0)),
            _const_spec((None, steps, 8, A_W), lambda i: (l, 0, 0, 0)),
            _const_spec((None, 8, A_W), lambda i: (l, 0, 0)),
            _const_spec((None, A_W, D), lambda i: (l, 0, 0)),
        ],
        out_specs=[tok(D), tok(D), tok(QKV_W), tok(A_W)],
        out_shape=[
            jax.ShapeDtypeStruct((rows, D), BF16),
            jax.ShapeDtypeStruct((rows, D), BF16),
            jax.ShapeDtypeStruct((rows, QKV_W), F32),
            jax.ShapeDtypeStruct((rows, A_W), F32),
        ],
        compiler_params=_params("arbitrary"),
        name="mix_in_sample",
    )(x, mod, norm_g, w_in, ln_g, ln_b, c_sp, b_sp, w_a2d)


def _band_attn_kernel(qkv_ref, o_ref, lse_ref, *, span):
    n_res, length, _ = qkv_ref.shape
    nb = length // CHUNK
    kw = CHUNK if nb == 1 else 2 * CHUNK
    head = _head_of_lane()
    qi = lax.broadcasted_iota(jnp.int32, (CHUNK, kw), 0)
    ki = lax.broadcasted_iota(jnp.int32, (CHUNK, kw), 1)
    ones = jnp.ones((kw, LANES), BF16)

    def unit(u, carry):
        r = u // nb
        b = u % nb
        q0 = pl.multiple_of(b * CHUNK, CHUNK)
        k0 = pl.multiple_of(jnp.maximum(b - 1, 0) * CHUNK, CHUNK)
        q = qkv_ref[r, pl.ds(q0, CHUNK), 0:GROUP_W]
        k = qkv_ref[r, pl.ds(k0, kw), GROUP_W:2 * GROUP_W]
        v = qkv_ref[r, pl.ds(k0, kw), 2 * GROUP_W:QKV_G]
        dist = (q0 - k0) + qi - ki
        valid = (dist >= 0) & (dist <= span)
        qs = jnp.concatenate([jnp.where(head == hh, q, jnp.zeros_like(q)) for hh in range(HEADS)], axis=0)
        s = _dot_nt(qs, k).reshape(HEADS, CHUNK, kw)
        s = jnp.where(valid[None], s, NEG).reshape(HEADS * CHUNK, kw)
        m = jnp.max(s, axis=-1, keepdims=True)
        p = jnp.exp(s - m).astype(BF16)
        z = _dot(p, ones)
        o = _dot(p, v)
        zinv = 1.0 / z
        o = o * jnp.concatenate([zinv, zinv], axis=-1)
        lse = m + jnp.log(z)
        lse = jnp.concatenate([lse, lse], axis=-1)
        acc = o[0:CHUNK]
        lacc = lse[0:CHUNK]
        for hh in range(1, HEADS):
            rows = slice(hh * CHUNK, (hh + 1) * CHUNK)
            acc = jnp.where(head == hh, o[rows], acc)
            lacc = jnp.where(head == hh, lse[rows], lacc)
        o_ref[r, pl.ds(q0, CHUNK), :] = acc.astype(o_ref.dtype)
        lse_ref[r, pl.ds(q0, CHUNK), :] = lacc
        return carry

    lax.fori_loop(0, n_res * nb, unit, 0, unroll=2)


def _band_attn(qkv, g):
    n, dil, length, _ = qkv.shape
    span = WINDOWS[g] // dil
    out_spec = pl.BlockSpec((None, dil, length, GROUP_W), lambda i: (i, 0, 0, 0))
    return pl.pallas_call(
        functools.partial(_band_attn_kernel, span=span),
        grid=(n,),
        in_specs=[pl.BlockSpec((None, dil, length, QKV_G), lambda i: (i, 0, 0, 0))],
        out_specs=[out_spec, out_spec],
        out_shape=[
            jax.ShapeDtypeStruct((n, dil, length, GROUP_W), BF16),
            jax.ShapeDtypeStruct((n, dil, length, GROUP_W), F32),
        ],
        compiler_params=_params("arbitrary"),
        name=f"band_attn_g{g}",
    )(qkv)


def _cached_group(q, k_new, v_new, cache_ref, dil):
    steps = q.shape[0]
    past = cache_ref.shape[1]
    head = _head_of_lane()
    q8 = jnp.concatenate([q * Q_SCALE, jnp.zeros((8 - steps, GROUP_W), F32)], axis=0).astype(BF16)
    qs = jnp.concatenate([jnp.where(head == hh, q8, jnp.zeros_like(q8)) for hh in range(HEADS)], axis=0)
    kt = cache_ref[0:GROUP_W, :].astype(BF16)
    vt = cache_ref[GROUP_W:2 * GROUP_W, :].astype(BF16)
    rows = HEADS * 8
    step_c = lax.broadcasted_iota(jnp.int32, (rows, past), 0) % steps
    pos = lax.broadcasted_iota(jnp.int32, (rows, past), 1)
    step_n = lax.broadcasted_iota(jnp.int32, (rows, steps), 0) % steps
    new = lax.broadcasted_iota(jnp.int32, (rows, steps), 1)
    if dil == 1:
        valid_c = pos >= step_c
        valid_n = new <= step_n
    else:
        valid_c = pos % dil == step_c
        valid_n = new == step_n
    s_c = jnp.where(valid_c, _dot(qs, kt), NEG)
    s_n = jnp.where(valid_n, _dot_nt(qs, k_new.astype(BF16)), NEG)
    m = jnp.maximum(jnp.max(s_c, axis=-1, keepdims=True), jnp.max(s_n, axis=-1, keepdims=True))
    p_c = jnp.exp(s_c - m)
    p_n = jnp.exp(s_n - m)
    z = jnp.sum(p_c, axis=-1, keepdims=True) + jnp.sum(p_n, axis=-1, keepdims=True)
    o = (_dot_nt(p_c.astype(BF16), vt) + _dot(p_n.astype(BF16), v_new.astype(BF16))) / z
    lse = jnp.broadcast_to(m + jnp.log(z), (rows, GROUP_W))
    acc = o[0:8]
    lacc = lse[0:8]
    for hh in range(1, HEADS):
        acc = jnp.where(head == hh, o[hh * 8:(hh + 1) * 8], acc)
        lacc = jnp.where(head == hh, lse[hh * 8:(hh + 1) * 8], lacc)
    return acc, lacc


def _cache_attn_kernel(qkv_ref, c0_ref, c1_ref, c2_ref, y_ref):
    nseq, steps, _ = qkv_ref.shape
    for i in range(nseq):
        outs = []
        for g, c_ref in enumerate((c0_ref, c1_ref, c2_ref)):
            o = g * QKV_G
            outs.append(_cached_group(qkv_ref[i, :, o:o + GROUP_W],
                                      qkv_ref[i, :, o + GROUP_W:o + 2 * GROUP_W],
                                      qkv_ref[i, :, o + 2 * GROUP_W:o + QKV_G],
                                      c_ref.at[i], DILS[g]))
        y_ref[i] = _merge(outs)[0:steps]


def _merge(groups):
    lmax = groups[0][1]
    for _, ls in groups[1:]:
        lmax = jnp.maximum(lmax, ls)
    num = 0.0
    den = 0.0
    for o, ls in groups:
        e = jnp.exp(ls - lmax)
        num = num + e * o
        den = den + e
    return num / den


def _cache_attn(l, qkv, caches):
    seqs, steps, _ = qkv.shape
    nb = SAMPLE_SEQS_PER_STEP
    views = []
    for g, c in enumerate(caches):
        past = c.shape[2]
        assert past == WINDOWS[g] and steps <= DILS[1], "window fully cached; steps fit one residue period"
        views.append(c.transpose(0, 1, 3, 4, 5, 2).reshape(DEPTH, seqs, 2 * GROUP_W, past))
    return pl.pallas_call(
        _cache_attn_kernel,
        grid=(seqs // nb,),
        in_specs=[pl.BlockSpec((nb, steps, QKV_W), lambda i: (i, 0, 0))] + [
            pl.BlockSpec((None, nb, 2 * GROUP_W, v.shape[3]), lambda i: (l, i, 0, 0)) for v in views
        ],
        out_specs=pl.BlockSpec((nb, steps, GROUP_W), lambda i: (i, 0, 0)),
        out_shape=jax.ShapeDtypeStruct((seqs, steps, GROUP_W), F32),
        compiler_params=_params("arbitrary"),
        name="cache_attn",
    )(qkv, *views)


def _mix_out_front(x, yb, za_ref, gb_ref, mods, ng_ref, wb_ref, wo_ref, wup_ref):
    sh2, sc2, gt1 = mods
    merged = za_ref[...].astype(F32) + gb_ref[...].astype(F32) * _dot(yb, wb_ref[...])
    x1 = x + gt1 * _rms(_dot(merged.astype(BF16), wo_ref[...]), ng_ref[1:2, :])
    h2 = (_rms(x1, ng_ref[2:3, :]) * (1.0 + sc2) + sh2).astype(BF16)
    return x1, _dot(h2, wup_ref[...])


def _mix_out_back(x1, conv, val, gt2, ng_ref, wdn_ref, y_ref):
    f = _dot((_gelu(conv) * val).astype(BF16), wdn_ref[...])
    y_ref[...] = x1 + gt2 * _rms(f, ng_ref[3:4, :])


def _interleave(blk_ref, scr, base, tm):
    dil = blk_ref.shape[0]
    if dil == 1:
        return blk_ref[0].astype(F32)
    for r in range(dil):
        x = blk_ref[r].astype(F32)
        for s in range(GROUP_W // LANES):
            scr[base + s, pl.ds(r, tm // dil, stride=dil), :] = x[:, s * LANES:(s + 1) * LANES]
    return jnp.concatenate([scr[base + s] for s in range(GROUP_W // LANES)], axis=-1)


def _mix_out_prompt_kernel(x_ref, za_ref, gb_ref, o0_ref, o1_ref, o2_ref, l0_ref, l1_ref, l2_ref,
                           mod_ref, ng_ref, wb_ref, wo_ref, wup_ref, cw_ref, cb_ref, wdn_ref,
                           y_ref, cs_ref, tail_scr, perm_scr):
    tm = x_ref.shape[0]

    @pl.when(pl.program_id(1) == 0)
    def _():
        tail_scr[...] = jnp.zeros_like(tail_scr)

    slabs = GROUP_W // LANES
    groups = []
    for g, (o_ref, l_ref) in enumerate(((o0_ref, l0_ref), (o1_ref, l1_ref), (o2_ref, l2_ref))):
        groups.append((_interleave(o_ref, perm_scr, 2 * g * slabs, tm),
                       _interleave(l_ref, perm_scr, (2 * g + 1) * slabs, tm)))
    yb = _merge(groups).astype(BF16)
    mods = (mod_ref[:, 3 * D:4 * D], mod_ref[:, 4 * D:5 * D], mod_ref[:, 2 * D:3 * D])
    x1, up = _mix_out_front(x_ref[...], yb, za_ref, gb_ref, mods, ng_ref, wb_ref, wo_ref, wup_ref)
    gate = up[:, :D_FF]
    prev = tail_scr[...]
    row = lax.broadcasted_iota(jnp.int32, (tm, D_FF), 0)
    g1 = jnp.where(row == 0, prev[7:8, :], pltpu.roll(gate, 1, 0))
    g2 = jnp.where(row == 0, prev[6:7, :], jnp.where(row == 1, prev[7:8, :], pltpu.roll(gate, 2, 0)))
    conv = cb_ref[...] + cw_ref[0:1, :] * g2 + cw_ref[1:2, :] * g1 + cw_ref[2:3, :] * gate
    tail_scr[...] = gate[tm - 8:, :]
    cs_ref[...] = gate[tm - (CONV_W - 1):, :]
    _mix_out_back(x1, conv, up[:, D_FF:], mod_ref[:, 5 * D:6 * D], ng_ref, wdn_ref, y_ref)


def _mix_out_prompt(l, x, za, gb, os, lses, mod, norm_g, w_b2d, w_out, w_up, conv_w, conv_b, w_down):
    n, t, _ = x.shape
    tm = TM_OUT
    tok = lambda w: pl.BlockSpec((None, tm, w), lambda i, j: (i, j, 0))
    res = lambda d: pl.BlockSpec((None, d, tm // d, GROUP_W), lambda i, j: (i, 0, j, 0))
    return pl.pallas_call(
        _mix_out_prompt_kernel,
        grid=(n, t // tm),
        in_specs=[
            tok(D), tok(D), tok(D),
            res(DILS[0]), res(DILS[1]), res(DILS[2]), res(DILS[0]), res(DILS[1]), res(DILS[2]),
            pl.BlockSpec((None, 1, 6 * D), lambda i, j: (i, 0, 0)),
            _const_spec((None, 4, D), lambda i, j: (l, 0, 0)),
            _const_spec((None, GROUP_W, D), lambda i, j: (l, 0, 0)),
            _const_spec((None, D, D), lambda i, j: (l, 0, 0)),
            _const_spec((None, D, 2 * D_FF), lambda i, j: (l, 0, 0)),
            _const_spec((None, CONV_W, D_FF), lambda i, j: (l, 0, 0)),
            _const_spec((None, 1, D_FF), lambda i, j: (l, 0, 0)),
            _const_spec((None, D_FF, D), lambda i, j: (l, 0, 0)),
        ],
        out_specs=[tok(D), pl.BlockSpec((None, CONV_W - 1, D_FF), lambda i, j: (i, 0, 0))],
        out_shape=[
            jax.ShapeDtypeStruct((n, t, D), F32),
            jax.ShapeDtypeStruct((n, CONV_W - 1, D_FF), F32),
        ],
        scratch_shapes=[pltpu.VMEM((8, D_FF), F32),
                        pltpu.VMEM((2 * N_GROUPS * GROUP_W // LANES, tm, LANES), F32)],
        compiler_params=_params("arbitrary", "arbitrary"),
        name="mix_out_prompt",
    )(x, za, gb, *os, *lses, mod, norm_g, w_b2d, w_out, w_up, conv_w, conv_b, w_down)


def _mix_out_sample_kernel(x_ref, za_ref, gb_ref, yb_ref, moda_ref, modb_ref, f1_ref, f2_ref,
                           ng_ref, wb_ref, wo_ref, wup_ref, cw_ref, cb_ref, wdn_ref,
                           y_ref, gate_ref, *, steps):
    tm = x_ref.shape[0]
    mods = (moda_ref[:, D:2 * D], modb_ref[:, 0:D], moda_ref[:, 0:D])
    x1, up = _mix_out_front(x_ref[...], yb_ref[...].astype(BF16), za_ref, gb_ref, mods,
                            ng_ref, wb_ref, wo_ref, wup_ref)
    gate = up[:, :D_FF]
    gate_ref[...] = gate
    step = lax.broadcasted_iota(jnp.int32, (tm, D_FF), 0) % steps
    g1 = jnp.where(step >= 1, pltpu.roll(gate, 1, 0), f1_ref[...])
    g2 = jnp.where(step >= 2, pltpu.roll(gate, 2, 0), f2_ref[...])
    conv = cb_ref[...] + cw_ref[0:1, :] * g2 + cw_ref[1:2, :] * g1 + cw_ref[2:3, :] * gate
    _mix_out_back(x1, conv, up[:, D_FF:], modb_ref[:, D:2 * D], ng_ref, wdn_ref, y_ref)


def _mix_out_sample(l, x, za, gb, yb, mod, fill1, fill2, norm_g, w_b2d, w_out, w_up, conv_w, conv_b,
                    w_down, steps):
    rows = x.shape[0]
    tm = TM_SAMPLE
    tok = lambda w: pl.BlockSpec((tm, w), lambda i: (i, 0))
    return pl.pallas_call(
        functools.partial(_mix_out_sample_kernel, steps=steps),
        grid=(rows // tm,),
        in_specs=[
            tok(D), tok(D), tok(D), tok(GROUP_W),
            pl.BlockSpec((tm, 2 * D), lambda i: (i, 1)),
            pl.BlockSpec((tm, 2 * D), lambda i: (i, 2)),
            tok(D_FF), tok(D_FF),
            _const_spec((None, 4, D), lambda i: (l, 0, 0)),
            _const_spec((None, GROUP_W, D), lambda i: (l, 0, 0)),
            _const_spec((None, D, D), lambda i: (l, 0, 0)),
            _const_spec((None, D, 2 * D_FF), lambda i: (l, 0, 0)),
            _const_spec((None, CONV_W, D_FF), lambda i: (l, 0, 0)),
            _const_spec((None, 1, D_FF), lambda i: (l, 0, 0)),
            _const_spec((None, D_FF, D), lambda i: (l, 0, 0)),
        ],
        out_specs=[tok(D), tok(D_FF)],
        out_shape=[
            jax.ShapeDtypeStruct((rows, D), F32),
            jax.ShapeDtypeStruct((rows, D_FF), F32),
        ],
        compiler_params=_params("arbitrary"),
        name="mix_out_sample",
    )(x, za, gb, yb, mod, mod, fill1, fill2, norm_g, w_b2d, w_out, w_up, conv_w, conv_b, w_down)


def _group_major_qkv(w_in):
    qkv = w_in[:, :, O_QKV:O_GATE].reshape(DEPTH, D, 3, N_GROUPS, GROUP_W)
    qkv = qkv.transpose(0, 1, 3, 2, 4).reshape(DEPTH, D, QKV_W)
    return jnp.concatenate([w_in[:, :, :O_QKV], qkv, w_in[:, :, O_GATE:]], axis=-1)


def _sample_spatial_tables(w_spatial, b_spatial, steps):
    step = jnp.arange(8) % steps
    coefs = []
    for k in range(steps):
        src = step - k
        w = w_spatial[:, :, step, jnp.maximum(src, 0)]
        coefs.append(jnp.where(src >= 0, w, 0.0))
    c_sp = jnp.stack(coefs, axis=1).transpose(0, 1, 3, 2)
    b_sp = b_spatial[:, :, step].transpose(0, 2, 1)
    return jnp.repeat(c_sp, CHUNK, axis=-1), jnp.repeat(b_sp, CHUNK, axis=-1)


def kernel(x_prompt, x_sample, cache_swa0, cache_swa1, cache_swa2, state_ffn_conv, c_prompt, c_sample,
           ada_w, ada_b, norm_g, w_in, ln_v_g, ln_v_b, w_spatial, b_spatial, w_a2d, w_b2d, w_out,
           w_up, conv_w, conv_b, w_down):
    n_p, t_p, _ = x_prompt.shape
    seqs, steps, _ = x_sample.shape
    rows_s = seqs * steps
    caches = (cache_swa0, cache_swa1, cache_swa2)

    w_in_b = _group_major_qkv(w_in).astype(BF16)
    w_a2d_b, w_b2d_b, w_out_b = w_a2d.astype(BF16), w_b2d.astype(BF16), w_out.astype(BF16)
    w_up_b, w_down_b = w_up.astype(BF16), w_down.astype(BF16)
    ln_g3 = ln_v_g.reshape(DEPTH, 1, A_W)
    ln_b3 = ln_v_b.reshape(DEPTH, 1, A_W)
    conv_b3 = conv_b.reshape(DEPTH, 1, D_FF)
    b_sp = jnp.repeat(b_spatial.transpose(0, 2, 1), CHUNK, axis=-1)
    c_sp_s, b_sp_s = _sample_spatial_tables(w_spatial, b_spatial, steps)

    mod = _modulation(jnp.concatenate([c_prompt, c_sample], axis=0), ada_w, ada_b)
    mod_p = mod[:, :n_p].reshape(DEPTH, n_p, 1, 6 * D)
    mod_s = jnp.repeat(mod[:, n_p:], steps, axis=1)

    pad_steps = ((0, 0), (0, 0), (0, steps - (CONV_W - 1)), (0, 0))
    fill2 = jnp.pad(state_ffn_conv, pad_steps).reshape(DEPTH, rows_s, D_FF)
    fill1 = jnp.pad(state_ffn_conv[:, :, 1:], ((0, 0), (0, 0), (0, steps - 1), (0, 0)))
    fill1 = fill1.reshape(DEPTH, rows_s, D_FF)

    y_p = x_prompt
    y_s = x_sample.reshape(rows_s, D)

    swa_p = [[] for _ in range(N_GROUPS)]
    swa_s = [[] for _ in range(N_GROUPS)]
    conv_p, conv_s, chunk_v_s = [], [], []
    for l in range(DEPTH):
        za, gb, qkv0, qkv1, qkv2, kv0, kv1, kv2 = _mix_in_prompt(
            l, y_p, mod_p[l], norm_g, w_in_b, ln_g3, ln_b3, w_spatial, b_sp, w_a2d_b)
        attn = [_band_attn(q, g) for g, q in enumerate((qkv0, qkv1, qkv2))]
        y_p, cst_p = _mix_out_prompt(
            l, y_p, za, gb, [a[0] for a in attn], [a[1] for a in attn], mod_p[l], norm_g,
            w_b2d_b, w_out_b, w_up_b, conv_w, conv_b3, w_down_b)
        for g, kv in enumerate((kv0, kv1, kv2)):
            swa_p[g].append(kv.reshape(n_p, kv.shape[1], 2, HEADS, HEAD_DIM))
        conv_p.append(cst_p)

        za_s, gb_s, qkv_s, vn_s = _mix_in_sample(
            l, y_s, mod_s[l], norm_g, w_in_b, ln_g3, ln_b3, c_sp_s, b_sp_s, w_a2d_b, steps)
        yb_s = _cache_attn(l, qkv_s.reshape(seqs, steps, QKV_W), caches)
        y_s, gate_s = _mix_out_sample(
            l, y_s, za_s, gb_s, yb_s.reshape(rows_s, GROUP_W), mod_s[l], fill1[l], fill2[l], norm_g,
            w_b2d_b, w_out_b, w_up_b, conv_w, conv_b3, w_down_b, steps)
        qkv_s6 = qkv_s.reshape(seqs, steps, N_GROUPS, 3, HEADS, HEAD_DIM)
        for g in range(N_GROUPS):
            swa_s[g].append(qkv_s6[:, :, g, 1:3])
        conv_s.append(gate_s.reshape(seqs, steps, D_FF)[:, steps - (CONV_W - 1):])
        chunk_v_s.append(vn_s.reshape(seqs, steps, A_W))

    return (y_p, y_s.reshape(seqs, steps, D),
            jnp.stack(swa_p[0]), jnp.stack(swa_p[1]), jnp.stack(swa_p[2]), jnp.stack(conv_p),
            jnp.stack(swa_s[0]), jnp.stack(swa_s[1]), jnp.stack(swa_s[2]), jnp.stack(conv_s),
            jnp.stack(chunk_v_s))
```

```python
import functools

import jax
import jax.numpy as jnp
from jax import lax
from jax.experimental import pallas as pl
from jax.experimental.pallas import tpu as pltpu

D = 1024
DEPTH = 4
CHUNK = 128
LANES = 128
A_W = 512
N_GROUPS = 3
DILS = (1, 4, 16)
WINDOWS = (128, 512, 2048)
GROUP_W = 256
HEAD_DIM = 64
HEADS = GROUP_W // HEAD_DIM
QKV_G = 3 * GROUP_W
QKV_W = N_GROUPS * QKV_G
D_FF = 2816
IN_W = 2 * A_W + QKV_W + 2 * D
O_QKV = 2 * A_W
O_GATE = O_QKV + QKV_W
CONV_W = 3
EPS = 1e-6
NEG = -1e30
Q_SCALE = HEAD_DIM ** -0.5

TM = 256
TM_OUT = 256
COL_CHUNK = 256
FF_CHUNK = D_FF
TM_SAMPLE = 128
VMEM_LIMIT = 56 * 1024 * 1024

F32 = jnp.float32
BF16 = jnp.bfloat16
_NT = (((1,), (1,)), ((), ()))


def _dot(a, b):
    return jnp.dot(a, b, preferred_element_type=F32)


def _dot_nt(a, b):
    return lax.dot_general(a, b, _NT, preferred_element_type=F32)


def _rms(x, g):
    return x * lax.rsqrt(jnp.mean(x * x, axis=-1, keepdims=True) + EPS) * g


def _gelu(x):
    return jax.nn.gelu(x)


def _layer_norm(x, g, b):
    mu = jnp.mean(x, axis=-1, keepdims=True)
    xc = x - mu
    var = jnp.mean(xc * xc, axis=-1, keepdims=True)
    return xc * lax.rsqrt(var + EPS) * g + b


def _const_spec(shape, index_map):
    return pl.BlockSpec(shape, index_map, pipeline_mode=pl.Buffered(1))


def _params(*sem):
    return pltpu.CompilerParams(dimension_semantics=sem, vmem_limit_bytes=VMEM_LIMIT)


def _head_of_lane():
    return lax.broadcasted_iota(jnp.int32, (1, GROUP_W), 1) // HEAD_DIM


def _mod_kernel(c_ref, w_ref, b_ref, o_ref):
    c = c_ref[...]
    s = (c * jax.nn.sigmoid(c)).astype(BF16)
    o_ref[...] = _dot(s, w_ref[...].astype(BF16)) + b_ref[...]


def _modulation(c_all, ada_w, ada_b):
    rows = c_all.shape[0]
    tn = 1536
    return pl.pallas_call(
        _mod_kernel,
        grid=(DEPTH, 6 * D // tn),
        in_specs=[
            pl.BlockSpec((rows, D), lambda l, j: (0, 0)),
            pl.BlockSpec((None, D, tn), lambda l, j: (l, 0, j)),
            pl.BlockSpec((None, 1, tn), lambda l, j: (l, 0, j)),
        ],
        out_specs=pl.BlockSpec((None, rows, tn), lambda l, j: (l, 0, j)),
        out_shape=jax.ShapeDtypeStruct((DEPTH, rows, 6 * D), F32),
        compiler_params=_params("arbitrary", "arbitrary"),
        name="adaln_mod",
    )(c_all, ada_w, ada_b.reshape(DEPTH, 1, 6 * D))


def _project_in(x, sh1, sc1, ng_ref, win_ref, lng_ref, lnb_ref):
    h = (_rms(x, ng_ref[0:1, :]) * (1.0 + sc1) + sh1).astype(BF16)
    puv = _dot(h, win_ref[:, 0:O_QKV])
    gu = _gelu(puv[:, :A_W])
    vn = _layer_norm(_gelu(puv[:, A_W:]), lng_ref[...], lnb_ref[...])
    return h, gu, vn


def _gates_out(h, ya, win_ref, wa_ref, za_ref, gb_ref):
    for c in range(D // COL_CHUNK):
        cols = slice(c * COL_CHUNK, (c + 1) * COL_CHUNK)
        ga = jax.nn.sigmoid(_dot(h, win_ref[:, O_GATE + c * COL_CHUNK:O_GATE + (c + 1) * COL_CHUNK]))
        za_ref[:, cols] = (ga * _dot(ya, wa_ref[:, cols])).astype(za_ref.dtype)
        o = O_GATE + D + c * COL_CHUNK
        gb_ref[:, cols] = jax.nn.sigmoid(_dot(h, win_ref[:, o:o + COL_CHUNK])).astype(gb_ref.dtype)


def _qkv_col(which, g):
    return (which * N_GROUPS + g) * GROUP_W


def _mix_in_prompt_kernel(x_ref, mod_ref, ng_ref, win_ref, lng_ref, lnb_ref, ws_ref, bsp_ref, wa_ref,
                          qkvs_ref, c0_ref, c1_ref, c2_ref,
                          za_ref, gb_ref, qkv0_ref, qkv1_ref, qkv2_ref, kv0_ref, kv1_ref, kv2_ref, ys_ref,
                          ya_scr, perm_scr):
    _cache_attn_body(qkvs_ref, (c0_ref, c1_ref, c2_ref), ys_ref)
    tm = x_ref.shape[0]
    h, gu, vn = _project_in(x_ref[...], mod_ref[:, 0:D], mod_ref[:, D:2 * D],
                            ng_ref, win_ref, lng_ref, lnb_ref)
    vnb = vn.astype(BF16)
    ti = lax.broadcasted_iota(jnp.int32, (CHUNK, CHUNK), 0)
    si = lax.broadcasted_iota(jnp.int32, (CHUNK, CHUNK), 1)
    causal = si <= ti
    for g in range(A_W // CHUNK):
        wg = jnp.where(causal, ws_ref[g], 0.0).astype(BF16)
        for c in range(tm // CHUNK):
            rows = slice(c * CHUNK, (c + 1) * CHUNK)
            cols = slice(g * CHUNK, (g + 1) * CHUNK)
            mix = _dot(wg, vnb[rows, cols]) + bsp_ref[:, cols]
            ya_scr[rows, cols] = (gu[rows, cols] * mix).astype(BF16)
    _gates_out(h, ya_scr[...], win_ref, wa_ref, za_ref, gb_ref)

    pq = _dot(h, win_ref[:, O_QKV:O_GATE])
    for g, kv_ref in enumerate((kv0_ref, kv1_ref, kv2_ref)):
        first = tm - kv_ref.shape[0]
        for which in (1, 2):
            col = _qkv_col(which, g)
            kv_ref[:, (which - 1) * GROUP_W:which * GROUP_W] = pq[first:, col:col + GROUP_W]
    halves = GROUP_W // LANES
    for g, out_ref in enumerate((qkv0_ref, qkv1_ref, qkv2_ref)):
        dil = DILS[g]
        for which in range(3):
            scale = Q_SCALE if which == 0 else 1.0
            col = _qkv_col(which, g)
            if dil == 1:
                out_ref[0, :, which * GROUP_W:(which + 1) * GROUP_W] = (
                    pq[:, col:col + GROUP_W] * scale).astype(BF16)
                continue
            for s in range(halves):
                perm_scr[which * halves + s] = pq[:, col + s * LANES:col + (s + 1) * LANES] * scale
        if dil == 1:
            continue
        for r in range(dil):
            for s in range(3 * halves):
                out_ref[r, :, s * LANES:(s + 1) * LANES] = (
                    perm_scr[s, pl.ds(r, tm // dil, stride=dil), :].astype(BF16))


def _mix_in_prompt(l, x, mod, norm_g, w_in, ln_g, ln_b, w_sp, b_sp, w_a2d, qkv_s, caches):
    n, t, _ = x.shape
    steps = t // TM
    keep = [min(w, t) for w in WINDOWS]
    kv_rows = [min(k, TM) for k in keep]
    first = [(t - k) // TM for k in keep]
    seqs, steps_s, _ = qkv_s.shape
    nb = seqs // (n * steps)
    assert nb * n * steps == seqs
    views = _cache_views(caches, steps_s)

    def kv_spec(g):
        return pl.BlockSpec((None, kv_rows[g], 2 * GROUP_W),
                            lambda i, j, g=g: (i, jnp.maximum(j - first[g], 0), 0))

    def qkv_spec(g):
        return pl.BlockSpec((None, DILS[g], TM // DILS[g], QKV_G), lambda i, j: (i, 0, j, 0))

    tok = lambda w: pl.BlockSpec((None, TM, w), lambda i, j: (i, j, 0))
    return pl.pallas_call(
        _mix_in_prompt_kernel,
        grid=(n, steps),
        in_specs=[
            tok(D),
            pl.BlockSpec((None, 1, 6 * D), lambda i, j: (i, 0, 0)),
            _const_spec((None, 4, D), lambda i, j: (l, 0, 0)),
            _const_spec((None, D, IN_W), lambda i, j: (l, 0, 0)),
            _const_spec((None, 1, A_W), lambda i, j: (l, 0, 0)),
            _const_spec((None, 1, A_W), lambda i, j: (l, 0, 0)),
            _const_spec((None, 4, CHUNK, CHUNK), lambda i, j: (l, 0, 0, 0)),
            _const_spec((None, CHUNK, A_W), lambda i, j: (l, 0, 0)),
            _const_spec((None, A_W, D), lambda i, j: (l, 0, 0)),
            pl.BlockSpec((nb, steps_s, QKV_W), lambda i, j: (i * steps + j, 0, 0)),
        ] + [
            pl.BlockSpec((None, nb, 2 * GROUP_W, v.shape[3]), lambda i, j: (l, i * steps + j, 0, 0))
            for v in views
        ],
        out_specs=[tok(D), tok(D), qkv_spec(0), qkv_spec(1), qkv_spec(2),
                   kv_spec(0), kv_spec(1), kv_spec(2),
                   pl.BlockSpec((nb, steps_s, GROUP_W), lambda i, j: (i * steps + j, 0, 0))],
        out_shape=[
            jax.ShapeDtypeStruct((n, t, D), BF16),
            jax.ShapeDtypeStruct((n, t, D), BF16),
        ] + [jax.ShapeDtypeStruct((n, d, t // d, QKV_G), BF16) for d in DILS] + [
            jax.ShapeDtypeStruct((n, k, 2 * GROUP_W), F32) for k in keep
        ] + [jax.ShapeDtypeStruct((seqs, steps_s, GROUP_W), F32)],
        scratch_shapes=[pltpu.VMEM((TM, A_W), BF16), pltpu.VMEM((QKV_G // LANES, TM, LANES), F32)],
        compiler_params=_params("arbitrary", "arbitrary"),
        name="mix_in_prompt",
    )(x, mod, norm_g, w_in, ln_g, ln_b, w_sp, b_sp, w_a2d, qkv_s, *views)


def _mix_in_sample_kernel(x_ref, mod_ref, ng_ref, win_ref, lng_ref, lnb_ref, csp_ref, bsp_ref, wa_ref,
                          za_ref, gb_ref, qkv_ref, vn_ref, *, steps):
    tm = x_ref.shape[0]
    h, gu, vn = _project_in(x_ref[...], mod_ref[:, 0:D], mod_ref[:, D:2 * D],
                            ng_ref, win_ref, lng_ref, lnb_ref)
    vn_ref[...] = vn
    mix = jnp.zeros((tm // 8, 8, A_W), F32) + bsp_ref[...][None]
    for k in range(steps):
        prev = vn if k == 0 else pltpu.roll(vn, k, 0)
        mix = mix + csp_ref[k][None] * prev.reshape(tm // 8, 8, A_W)
    ya = (gu * mix.reshape(tm, A_W)).astype(BF16)
    _gates_out(h, ya, win_ref, wa_ref, za_ref, gb_ref)
    qkv_ref[...] = _dot(h, win_ref[:, O_QKV:O_GATE])


def _mix_in_sample(l, x, mod, norm_g, w_in, ln_g, ln_b, c_sp, b_sp, w_a2d, steps):
    rows = x.shape[0]
    tm = TM_SAMPLE
    tok = lambda w: pl.BlockSpec((tm, w), lambda i: (i, 0))
    return pl.pallas_call(
        functools.partial(_mix_in_sample_kernel, steps=steps),
        grid=(rows // tm,),
        in_specs=[
            tok(D),
            pl.BlockSpec((None, tm, 2 * D), lambda i: (l, i, 0)),
            _const_spec((None, 4, D), lambda i: (l, 0, 0)),
            _const_spec((None, D, IN_W), lambda i: (l, 0, 0)),
            _const_spec((None, 1, A_W), lambda i: (l, 0, 0)),
            _const_spec((None, 1, A_W), lambda i: (l, 0, 0)),
            _const_spec((None, steps, 8, A_W), lambda i: (l, 0, 0, 0)),
            _const_spec((None, 8, A_W), lambda i: (l, 0, 0)),
            _const_spec((None, A_W, D), lambda i: (l, 0, 0)),
        ],
        out_specs=[tok(D), tok(D), tok(QKV_W), tok(A_W)],
        out_shape=[
            jax.ShapeDtypeStruct((rows, D), BF16),
            jax.ShapeDtypeStruct((rows, D), BF16),
            jax.ShapeDtypeStruct((rows, QKV_W), F32),
            jax.ShapeDtypeStruct((rows, A_W), F32),
        ],
        compiler_params=_params("arbitrary"),
        name="mix_in_sample",
    )(x, mod, norm_g, w_in, ln_g, ln_b, c_sp, b_sp, w_a2d)


def _band_attn_kernel(qkv_ref, o_ref, lse_ref, *, span):
    n_res, length, _ = qkv_ref.shape
    nb = length // CHUNK
    kw = CHUNK if nb == 1 else 2 * CHUNK
    head = _head_of_lane()
    qi = lax.broadcasted_iota(jnp.int32, (CHUNK, kw), 0)
    ki = lax.broadcasted_iota(jnp.int32, (CHUNK, kw), 1)
    ones = jnp.ones((kw, LANES), BF16)

    def unit(u, carry):
        r = u // nb
        b = u % nb
        q0 = pl.multiple_of(b * CHUNK, CHUNK)
        k0 = pl.multiple_of(jnp.maximum(b - 1, 0) * CHUNK, CHUNK)
        q = qkv_ref[r, pl.ds(q0, CHUNK), 0:GROUP_W]
        k = qkv_ref[r, pl.ds(k0, kw), GROUP_W:2 * GROUP_W]
        v = qkv_ref[r, pl.ds(k0, kw), 2 * GROUP_W:QKV_G]
        dist = (q0 - k0) + qi - ki
        valid = (dist >= 0) & (dist <= span)
        qs = jnp.concatenate([jnp.where(head == hh, q, jnp.zeros_like(q)) for hh in range(HEADS)], axis=0)
        s = _dot_nt(qs, k).reshape(HEADS, CHUNK, kw)
        s = jnp.where(valid[None], s, NEG).reshape(HEADS * CHUNK, kw)
        m = jnp.max(s, axis=-1, keepdims=True)
        p = jnp.exp(s - m).astype(BF16)
        z = _dot(p, ones)
        o = _dot(p, v)
        zinv = 1.0 / z
        o = o * jnp.concatenate([zinv, zinv], axis=-1)
        lse = m + jnp.log(z)
        lse = jnp.concatenate([lse, lse], axis=-1)
        acc = o[0:CHUNK]
        lacc = lse[0:CHUNK]
        for hh in range(1, HEADS):
            rows = slice(hh * CHUNK, (hh + 1) * CHUNK)
            acc = jnp.where(head == hh, o[rows], acc)
            lacc = jnp.where(head == hh, lse[rows], lacc)
        o_ref[r, pl.ds(q0, CHUNK), :] = acc.astype(o_ref.dtype)
        lse_ref[r, pl.ds(q0, CHUNK), :] = lacc
        return carry

    lax.fori_loop(0, n_res * nb, unit, 0, unroll=2)


def _band_attn(qkv, g):
    n, dil, length, _ = qkv.shape
    span = WINDOWS[g] // dil
    out_spec = pl.BlockSpec((None, dil, length, GROUP_W), lambda i: (i, 0, 0, 0))
    return pl.pallas_call(
        functools.partial(_band_attn_kernel, span=span),
        grid=(n,),
        in_specs=[pl.BlockSpec((None, dil, length, QKV_G), lambda i: (i, 0, 0, 0))],
        out_specs=[out_spec, out_spec],
        out_shape=[
            jax.ShapeDtypeStruct((n, dil, length, GROUP_W), BF16),
            jax.ShapeDtypeStruct((n, dil, length, GROUP_W), F32),
        ],
        compiler_params=_params("arbitrary"),
        name=f"band_attn_g{g}",
    )(qkv)


def _cached_group(q, k_new, v_new, cache_ref, dil):
    steps = q.shape[0]
    past = cache_ref.shape[1]
    head = _head_of_lane()
    q8 = jnp.concatenate([q * Q_SCALE, jnp.zeros((8 - steps, GROUP_W), F32)], axis=0).astype(BF16)
    qs = jnp.concatenate([jnp.where(head == hh, q8, jnp.zeros_like(q8)) for hh in range(HEADS)], axis=0)
    kt = cache_ref[0:GROUP_W, :].astype(BF16)
    vt = cache_ref[GROUP_W:2 * GROUP_W, :].astype(BF16)
    rows = HEADS * 8
    step_c = lax.broadcasted_iota(jnp.int32, (rows, past), 0) % steps
    pos = lax.broadcasted_iota(jnp.int32, (rows, past), 1)
    step_n = lax.broadcasted_iota(jnp.int32, (rows, steps), 0) % steps
    new = lax.broadcasted_iota(jnp.int32, (rows, steps), 1)
    if dil == 1:
        valid_c = pos >= step_c
        valid_n = new <= step_n
    else:
        valid_c = pos % dil == step_c
        valid_n = new == step_n
    s_c = jnp.where(valid_c, _dot(qs, kt), NEG)
    s_n = jnp.where(valid_n, _dot_nt(qs, k_new.astype(BF16)), NEG)
    m = jnp.maximum(jnp.max(s_c, axis=-1, keepdims=True), jnp.max(s_n, axis=-1, keepdims=True))
    p_c = jnp.exp(s_c - m)
    p_n = jnp.exp(s_n - m)
    z = jnp.sum(p_c, axis=-1, keepdims=True) + jnp.sum(p_n, axis=-1, keepdims=True)
    o = (_dot_nt(p_c.astype(BF16), vt) + _dot(p_n.astype(BF16), v_new.astype(BF16))) / z
    lse = jnp.broadcast_to(m + jnp.log(z), (rows, GROUP_W))
    acc = o[0:8]
    lacc = lse[0:8]
    for hh in range(1, HEADS):
        acc = jnp.where(head == hh, o[hh * 8:(hh + 1) * 8], acc)
        lacc = jnp.where(head == hh, lse[hh * 8:(hh + 1) * 8], lacc)
    return acc, lacc


def _cache_attn_body(qkv_ref, cache_refs, y_ref):
    nseq, steps, _ = qkv_ref.shape
    for i in range(nseq):
        outs = []
        for g, c_ref in enumerate(cache_refs):
            q, k_new, v_new = (qkv_ref[i, :, _qkv_col(which, g):_qkv_col(which, g) + GROUP_W]
                               for which in range(3))
            outs.append(_cached_group(q, k_new, v_new, c_ref.at[i], DILS[g]))
        y_ref[i] = _merge(outs)[0:steps]


def _merge(groups):
    lmax = groups[0][1]
    for _, ls in groups[1:]:
        lmax = jnp.maximum(lmax, ls)
    num = 0.0
    den = 0.0
    for o, ls in groups:
        e = jnp.exp(ls - lmax)
        num = num + e * o
        den = den + e
    return num / den


def _cache_views(caches, steps):
    views = []
    for g, c in enumerate(caches):
        depth, seqs, past = c.shape[:3]
        assert past == WINDOWS[g] and steps <= DILS[1], "window fully cached; steps fit one residue period"
        views.append(c.transpose(0, 1, 3, 4, 5, 2).reshape(depth, seqs, 2 * GROUP_W, past))
    return views


def _mix_out_front(x, yb, za_ref, gb_ref, mods, ng_ref, wb_ref, wo_ref):
    sh2, sc2, gt1 = mods
    merged = za_ref[...].astype(F32) + gb_ref[...].astype(F32) * _dot(yb, wb_ref[...])
    x1 = x + gt1 * _rms(_dot(merged.astype(BF16), wo_ref[...]), ng_ref[1:2, :])
    h2 = (_rms(x1, ng_ref[2:3, :]) * (1.0 + sc2) + sh2).astype(BF16)
    return x1, h2


def _conv_glu(h2, earlier_rows, wup_ref, cw_ref, cb_ref, wdn_ref):
    f = None
    for c in range(D_FF // FF_CHUNK):
        cols = slice(c * FF_CHUNK, (c + 1) * FF_CHUNK)
        gate = _dot(h2, wup_ref[:, cols])
        val = _dot(h2, wup_ref[:, D_FF + c * FF_CHUNK:D_FF + (c + 1) * FF_CHUNK])
        g1, g2 = earlier_rows(cols, gate)
        conv = cb_ref[:, cols] + cw_ref[0:1, cols] * g2 + cw_ref[1:2, cols] * g1 + cw_ref[2:3, cols] * gate
        part = _dot((_gelu(conv) * val).astype(BF16), wdn_ref[cols, :])
        f = part if f is None else f + part
    return f


def _interleave(blk_ref, scr, base, tm):
    dil = blk_ref.shape[0]
    if dil == 1:
        return blk_ref[0].astype(F32)
    for r in range(dil):
        x = blk_ref[r].astype(F32)
        for s in range(GROUP_W // LANES):
            scr[base + s, pl.ds(r, tm // dil, stride=dil), :] = x[:, s * LANES:(s + 1) * LANES]
    return jnp.concatenate([scr[base + s] for s in range(GROUP_W // LANES)], axis=-1)


def _mix_out_prompt_kernel(x_ref, za_ref, gb_ref, o0_ref, o1_ref, o2_ref, l0_ref, l1_ref, l2_ref,
                           mod_ref, ng_ref, wb_ref, wo_ref, wup_ref, cw_ref, cb_ref, wdn_ref,
                           y_ref, cs_ref, tail_scr, perm_scr):
    tm = x_ref.shape[0]

    @pl.when(pl.program_id(1) == 0)
    def _():
        tail_scr[...] = jnp.zeros_like(tail_scr)

    slabs = GROUP_W // LANES
    groups = []
    for g, (o_ref, l_ref) in enumerate(((o0_ref, l0_ref), (o1_ref, l1_ref), (o2_ref, l2_ref))):
        groups.append((_interleave(o_ref, perm_scr, 2 * g * slabs, tm),
                       _interleave(l_ref, perm_scr, (2 * g + 1) * slabs, tm)))
    yb = _merge(groups).astype(BF16)
    mods = (mod_ref[:, 3 * D:4 * D], mod_ref[:, 4 * D:5 * D], mod_ref[:, 2 * D:3 * D])
    x1, h2 = _mix_out_front(x_ref[...], yb, za_ref, gb_ref, mods, ng_ref, wb_ref, wo_ref)
    row = lax.broadcasted_iota(jnp.int32, (tm, FF_CHUNK), 0)

    def earlier_rows(cols, gate):
        prev = tail_scr[:, cols]
        g1 = jnp.where(row == 0, prev[7:8, :], pltpu.roll(gate, 1, 0))
        g2 = jnp.where(row == 0, prev[6:7, :], jnp.where(row == 1, prev[7:8, :], pltpu.roll(gate, 2, 0)))
        tail_scr[:, cols] = gate[tm - 8:, :]
        cs_ref[:, cols] = gate[tm - (CONV_W - 1):, :]
        return g1, g2

    f = _conv_glu(h2, earlier_rows, wup_ref, cw_ref, cb_ref, wdn_ref)
    y_ref[...] = x1 + mod_ref[:, 5 * D:6 * D] * _rms(f, ng_ref[3:4, :])


def _mix_out_prompt(l, x, za, gb, os, lses, mod, norm_g, w_b2d, w_out, w_up, conv_w, conv_b, w_down):
    n, t, _ = x.shape
    tm = TM_OUT
    tok = lambda w: pl.BlockSpec((None, tm, w), lambda i, j: (i, j, 0))
    res = lambda d: pl.BlockSpec((None, d, tm // d, GROUP_W), lambda i, j: (i, 0, j, 0))
    return pl.pallas_call(
        _mix_out_prompt_kernel,
        grid=(n, t // tm),
        in_specs=[
            tok(D), tok(D), tok(D),
            res(DILS[0]), res(DILS[1]), res(DILS[2]), res(DILS[0]), res(DILS[1]), res(DILS[2]),
            pl.BlockSpec((None, 1, 6 * D), lambda i, j: (i, 0, 0)),
            _const_spec((None, 4, D), lambda i, j: (l, 0, 0)),
            _const_spec((None, GROUP_W, D), lambda i, j: (l, 0, 0)),
            _const_spec((None, D, D), lambda i, j: (l, 0, 0)),
            _const_spec((None, D, 2 * D_FF), lambda i, j: (l, 0, 0)),
            _const_spec((None, CONV_W, D_FF), lambda i, j: (l, 0, 0)),
            _const_spec((None, 1, D_FF), lambda i, j: (l, 0, 0)),
            _const_spec((None, D_FF, D), lambda i, j: (l, 0, 0)),
        ],
        out_specs=[tok(D), pl.BlockSpec((None, CONV_W - 1, D_FF), lambda i, j: (i, 0, 0))],
        out_shape=[
            jax.ShapeDtypeStruct((n, t, D), F32),
            jax.ShapeDtypeStruct((n, CONV_W - 1, D_FF), F32),
        ],
        scratch_shapes=[pltpu.VMEM((8, D_FF), F32),
                        pltpu.VMEM((2 * N_GROUPS * GROUP_W // LANES, tm, LANES), F32)],
        compiler_params=_params("arbitrary", "arbitrary"),
        name="mix_out_prompt",
    )(x, za, gb, *os, *lses, mod, norm_g, w_b2d, w_out, w_up, conv_w, conv_b, w_down)


def _mix_out_sample_kernel(x_ref, za_ref, gb_ref, yb_ref, moda_ref, modb_ref, st_ref,
                           ng_ref, wb_ref, wo_ref, wup_ref, cw_ref, cb_ref, wdn_ref,
                           y_ref, gate_ref, fill_scr, *, steps):
    tm = x_ref.shape[0]
    nseq = tm // steps
    slabs = D_FF // LANES
    fill_scr[...] = jnp.zeros_like(fill_scr)
    for s in range(slabs):
        older = st_ref[s, pl.ds(0, nseq, stride=CONV_W - 1), :]
        newer = st_ref[s, pl.ds(1, nseq, stride=CONV_W - 1), :]
        fill_scr[s, pl.ds(0, nseq, stride=steps), :] = newer
        fill_scr[slabs + s, pl.ds(0, nseq, stride=steps), :] = older
        fill_scr[slabs + s, pl.ds(1, nseq, stride=steps), :] = newer

    mods = (moda_ref[:, D:2 * D], modb_ref[:, 0:D], moda_ref[:, 0:D])
    x1, h2 = _mix_out_front(x_ref[...], yb_ref[...].astype(BF16), za_ref, gb_ref, mods,
                            ng_ref, wb_ref, wo_ref)
    step = lax.broadcasted_iota(jnp.int32, (tm, FF_CHUNK), 0) % steps

    def earlier_rows(cols, gate):
        gate_ref[:, cols] = gate
        first = cols.start // LANES
        fill = lambda base: jnp.concatenate(
            [fill_scr[base + first + s] for s in range(FF_CHUNK // LANES)], axis=-1)
        g1 = jnp.where(step >= 1, pltpu.roll(gate, 1, 0), fill(0))
        g2 = jnp.where(step >= 2, pltpu.roll(gate, 2, 0), fill(slabs))
        return g1, g2

    f = _conv_glu(h2, earlier_rows, wup_ref, cw_ref, cb_ref, wdn_ref)
    y_ref[...] = x1 + modb_ref[:, D:2 * D] * _rms(f, ng_ref[3:4, :])


def _mix_out_sample(l, x, za, gb, yb, mod, state, norm_g, w_b2d, w_out, w_up, conv_w, conv_b,
                    w_down, steps):
    rows = x.shape[0]
    tm = TM_SAMPLE
    slabs = D_FF // LANES
    st_rows = tm // steps * (CONV_W - 1)
    tok = lambda w: pl.BlockSpec((tm, w), lambda i: (i, 0))
    return pl.pallas_call(
        functools.partial(_mix_out_sample_kernel, steps=steps),
        grid=(rows // tm,),
        in_specs=[
            tok(D), tok(D), tok(D), tok(GROUP_W),
            pl.BlockSpec((None, tm, 2 * D), lambda i: (l, i, 1)),
            pl.BlockSpec((None, tm, 2 * D), lambda i: (l, i, 2)),
            pl.BlockSpec((None, slabs, st_rows, LANES), lambda i: (l, 0, i, 0)),
            _const_spec((None, 4, D), lambda i: (l, 0, 0)),
            _const_spec((None, GROUP_W, D), lambda i: (l, 0, 0)),
            _const_spec((None, D, D), lambda i: (l, 0, 0)),
            _const_spec((None, D, 2 * D_FF), lambda i: (l, 0, 0)),
            _const_spec((None, CONV_W, D_FF), lambda i: (l, 0, 0)),
            _const_spec((None, 1, D_FF), lambda i: (l, 0, 0)),
            _const_spec((None, D_FF, D), lambda i: (l, 0, 0)),
        ],
        out_specs=[tok(D), tok(D_FF)],
        out_shape=[
            jax.ShapeDtypeStruct((rows, D), F32),
            jax.ShapeDtypeStruct((rows, D_FF), F32),
        ],
        scratch_shapes=[pltpu.VMEM((2 * slabs, tm, LANES), F32)],
        compiler_params=_params("arbitrary"),
        name="mix_out_sample",
    )(x, za, gb, yb, mod, mod, state, norm_g, w_b2d, w_out, w_up, conv_w, conv_b, w_down)


def _sample_spatial_tables(w_spatial, b_spatial, steps):
    step = jnp.arange(8) % steps
    w_first = w_spatial[:, :, :steps, :steps]
    coefs = []
    for k in range(steps):
        src = step - k
        w = w_first[:, :, step, jnp.maximum(src, 0)]
        coefs.append(jnp.where(src >= 0, w, 0.0))
    c_sp = jnp.stack(coefs, axis=1).transpose(0, 1, 3, 2)
    b_sp = b_spatial[:, :, step].transpose(0, 2, 1)
    return jnp.repeat(c_sp, CHUNK, axis=-1), jnp.repeat(b_sp, CHUNK, axis=-1)


def kernel(x_prompt, x_sample, cache_swa0, cache_swa1, cache_swa2, state_ffn_conv, c_prompt, c_sample,
           ada_w, ada_b, norm_g, w_in, ln_v_g, ln_v_b, w_spatial, b_spatial, w_a2d, w_b2d, w_out,
           w_up, conv_w, conv_b, w_down):
    n_p, t_p, _ = x_prompt.shape
    seqs, steps, _ = x_sample.shape
    rows_s = seqs * steps
    caches = (cache_swa0, cache_swa1, cache_swa2)

    w_in_b = w_in.astype(BF16)
    w_a2d_b, w_b2d_b, w_out_b = w_a2d.astype(BF16), w_b2d.astype(BF16), w_out.astype(BF16)
    w_up_b, w_down_b = w_up.astype(BF16), w_down.astype(BF16)
    ln_g3 = ln_v_g.reshape(DEPTH, 1, A_W)
    ln_b3 = ln_v_b.reshape(DEPTH, 1, A_W)
    conv_b3 = conv_b.reshape(DEPTH, 1, D_FF)
    b_sp = jnp.repeat(b_spatial.transpose(0, 2, 1), CHUNK, axis=-1)
    c_sp_s, b_sp_s = _sample_spatial_tables(w_spatial, b_spatial, steps)

    c_rows = jnp.concatenate([jnp.repeat(c_sample, steps, axis=0), c_prompt], axis=0)
    mod = _modulation(c_rows, ada_w, ada_b)
    mod_p = mod[:, rows_s:].reshape(DEPTH, n_p, 1, 6 * D)

    state_slabs = state_ffn_conv.reshape(DEPTH, seqs * (CONV_W - 1), D_FF // LANES, LANES)
    state_slabs = state_slabs.transpose(0, 2, 1, 3)

    y_p = x_prompt
    y_s = x_sample.reshape(rows_s, D)

    swa_p = [[] for _ in range(N_GROUPS)]
    swa_s = [[] for _ in range(N_GROUPS)]
    conv_p, conv_s, chunk_v_s = [], [], []
    for l in range(DEPTH):
        za_s, gb_s, qkv_s, vn_s = _mix_in_sample(
            l, y_s, mod, norm_g, w_in_b, ln_g3, ln_b3, c_sp_s, b_sp_s, w_a2d_b, steps)

        za, gb, qkv0, qkv1, qkv2, kv0, kv1, kv2, yb_s = _mix_in_prompt(
            l, y_p, mod_p[l], norm_g, w_in_b, ln_g3, ln_b3, w_spatial, b_sp, w_a2d_b,
            qkv_s.reshape(seqs, steps, QKV_W), caches)
        attn = [_band_attn(q, g) for g, q in enumerate((qkv0, qkv1, qkv2))]
        y_p, cst_p = _mix_out_prompt(
            l, y_p, za, gb, [a[0] for a in attn], [a[1] for a in attn], mod_p[l], norm_g,
            w_b2d_b, w_out_b, w_up_b, conv_w, conv_b3, w_down_b)
        for g, kv in enumerate((kv0, kv1, kv2)):
            swa_p[g].append(kv.reshape(n_p, kv.shape[1], 2, HEADS, HEAD_DIM))
        conv_p.append(cst_p)

        y_s, gate_s = _mix_out_sample(
            l, y_s, za_s, gb_s, yb_s.reshape(rows_s, GROUP_W), mod, state_slabs, norm_g,
            w_b2d_b, w_out_b, w_up_b, conv_w, conv_b3, w_down_b, steps)
        qkv_s6 = qkv_s.reshape(seqs, steps, 3, N_GROUPS, HEADS, HEAD_DIM)
        for g in range(N_GROUPS):
            swa_s[g].append(qkv_s6[:, :, 1:3, g])
        conv_s.append(gate_s.reshape(seqs, steps, D_FF)[:, steps - (CONV_W - 1):])
        chunk_v_s.append(vn_s.reshape(seqs, steps, A_W))

    return (y_p, y_s.reshape(seqs, steps, D),
            jnp.stack(swa_p[0]), jnp.stack(swa_p[1]), jnp.stack(swa_p[2]), jnp.stack(conv_p),
            jnp.stack(swa_s[0]), jnp.stack(swa_s[1]), jnp.stack(swa_s[2]), jnp.stack(conv_s),
            jnp.stack(chunk_v_s))
```

```python
import functools

import jax
import jax.numpy as jnp
from jax import lax
from jax.experimental import pallas as pl
from jax.experimental.pallas import tpu as pltpu

D = 1024
DEPTH = 4
CHUNK = 128
LANES = 128
A_W = 512
N_GROUPS = 3
DILS = (1, 4, 16)
WINDOWS = (128, 512, 2048)
GROUP_W = 256
HEAD_DIM = 64
HEADS = GROUP_W // HEAD_DIM
QKV_G = 3 * GROUP_W
QKV_W = N_GROUPS * QKV_G
D_FF = 2816
IN_W = 2 * A_W + QKV_W + 2 * D
O_QKV = 2 * A_W
O_GATE = O_QKV + QKV_W
CONV_W = 3
EPS = 1e-6
NEG = -1e30
Q_SCALE = HEAD_DIM ** -0.5
LOG2_E = 1.4426950408889634
LN_2 = 0.6931471805599453

TM = 256
TM_OUT = 512
SUB_OUT = 256
COL_CHUNK = 256
FF_CHUNK = D_FF
TM_SAMPLE = 128
VMEM_LIMIT = 56 * 1024 * 1024

F32 = jnp.float32
BF16 = jnp.bfloat16
_NT = (((1,), (1,)), ((), ()))


def _dot(a, b):
    return jnp.dot(a, b, preferred_element_type=F32)


def _dot_nt(a, b):
    return lax.dot_general(a, b, _NT, preferred_element_type=F32)


def _rms(x, g):
    return x * lax.rsqrt(jnp.mean(x * x, axis=-1, keepdims=True) + EPS) * g


def _gelu(x):
    return jax.nn.gelu(x)


def _layer_norm(x, g, b):
    mu = jnp.mean(x, axis=-1, keepdims=True)
    xc = x - mu
    var = jnp.mean(xc * xc, axis=-1, keepdims=True)
    return xc * lax.rsqrt(var + EPS) * g + b


def _const_spec(shape, index_map):
    return pl.BlockSpec(shape, index_map, pipeline_mode=pl.Buffered(1))


def _params(*sem):
    return pltpu.CompilerParams(dimension_semantics=sem, vmem_limit_bytes=VMEM_LIMIT)


def _head_of_lane():
    return lax.broadcasted_iota(jnp.int32, (1, GROUP_W), 1) // HEAD_DIM


def _mod_kernel(c_ref, w_ref, b_ref, o_ref):
    c = c_ref[...]
    s = (c * jax.nn.sigmoid(c)).astype(BF16)
    o_ref[...] = _dot(s, w_ref[...].astype(BF16)) + b_ref[...]


def _modulation(c_all, ada_w, ada_b):
    rows = c_all.shape[0]
    tn = 1536
    return pl.pallas_call(
        _mod_kernel,
        grid=(DEPTH, 6 * D // tn),
        in_specs=[
            pl.BlockSpec((rows, D), lambda l, j: (0, 0)),
            pl.BlockSpec((None, D, tn), lambda l, j: (l, 0, j)),
            pl.BlockSpec((None, 1, tn), lambda l, j: (l, 0, j)),
        ],
        out_specs=pl.BlockSpec((None, rows, tn), lambda l, j: (l, 0, j)),
        out_shape=jax.ShapeDtypeStruct((DEPTH, rows, 6 * D), F32),
        compiler_params=_params("arbitrary", "arbitrary"),
        name="adaln_mod",
    )(c_all, ada_w, ada_b.reshape(DEPTH, 1, 6 * D))


def _project_in(x, sh1, sc1, ng_ref, win_ref, lng_ref, lnb_ref):
    h = (_rms(x, ng_ref[0:1, :]) * (1.0 + sc1) + sh1).astype(BF16)
    puv = _dot(h, win_ref[:, 0:O_QKV])
    gu = _gelu(puv[:, :A_W])
    vn = _layer_norm(_gelu(puv[:, A_W:]), lng_ref[...], lnb_ref[...])
    return h, gu, vn


def _gates_out(h, ya, win_ref, wa_ref, za_ref, gb_ref):
    for c in range(D // COL_CHUNK):
        cols = slice(c * COL_CHUNK, (c + 1) * COL_CHUNK)
        ga = jax.nn.sigmoid(_dot(h, win_ref[:, O_GATE + c * COL_CHUNK:O_GATE + (c + 1) * COL_CHUNK]))
        za_ref[:, cols] = (ga * _dot(ya, wa_ref[:, cols])).astype(za_ref.dtype)
        o = O_GATE + D + c * COL_CHUNK
        gb_ref[:, cols] = jax.nn.sigmoid(_dot(h, win_ref[:, o:o + COL_CHUNK])).astype(gb_ref.dtype)


def _qkv_col(which, g):
    return (which * N_GROUPS + g) * GROUP_W


def _mix_in_prompt_kernel(x_ref, mod_ref, ng_ref, win_ref, lng_ref, lnb_ref, ws_ref, bsp_ref, wa_ref,
                          qkvs_ref, c0_ref, c1_ref, c2_ref,
                          za_ref, gb_ref, qkv0_ref, qkv1_ref, qkv2_ref, kv0_ref, kv1_ref, kv2_ref, ys_ref,
                          ya_scr, perm_scr):
    _cache_attn_body(qkvs_ref, (c0_ref, c1_ref, c2_ref), ys_ref)
    tm = x_ref.shape[0]
    h, gu, vn = _project_in(x_ref[...], mod_ref[:, 0:D], mod_ref[:, D:2 * D],
                            ng_ref, win_ref, lng_ref, lnb_ref)
    vnb = vn.astype(BF16)
    ti = lax.broadcasted_iota(jnp.int32, (CHUNK, CHUNK), 0)
    si = lax.broadcasted_iota(jnp.int32, (CHUNK, CHUNK), 1)
    causal = si <= ti
    for g in range(A_W // CHUNK):
        wg = jnp.where(causal, ws_ref[g], 0.0).astype(BF16)
        for c in range(tm // CHUNK):
            rows = slice(c * CHUNK, (c + 1) * CHUNK)
            cols = slice(g * CHUNK, (g + 1) * CHUNK)
            mix = _dot(wg, vnb[rows, cols]) + bsp_ref[:, cols]
            ya_scr[rows, cols] = (gu[rows, cols] * mix).astype(BF16)
    _gates_out(h, ya_scr[...], win_ref, wa_ref, za_ref, gb_ref)

    pq = _dot(h, win_ref[:, O_QKV:O_GATE])
    for g, kv_ref in enumerate((kv0_ref, kv1_ref, kv2_ref)):
        first = tm - kv_ref.shape[0]
        for which in (1, 2):
            col = _qkv_col(which, g)
            kv_ref[:, (which - 1) * GROUP_W:which * GROUP_W] = pq[first:, col:col + GROUP_W]
    halves = GROUP_W // LANES
    for g, out_ref in enumerate((qkv0_ref, qkv1_ref, qkv2_ref)):
        dil = DILS[g]
        for which in range(3):
            scale = Q_SCALE * LOG2_E if which == 0 else 1.0
            col = _qkv_col(which, g)
            if dil == 1:
                out_ref[0, :, which * GROUP_W:(which + 1) * GROUP_W] = (
                    pq[:, col:col + GROUP_W] * scale).astype(BF16)
                continue
            for s in range(halves):
                perm_scr[which * halves + s] = pq[:, col + s * LANES:col + (s + 1) * LANES] * scale
        if dil == 1:
            continue
        for r in range(dil):
            for s in range(3 * halves):
                out_ref[r, :, s * LANES:(s + 1) * LANES] = (
                    perm_scr[s, pl.ds(r, tm // dil, stride=dil), :].astype(BF16))


def _mix_in_prompt(l, x, mod, norm_g, w_in, ln_g, ln_b, w_sp, b_sp, w_a2d, qkv_s, caches):
    n, t, _ = x.shape
    steps = t // TM
    keep = [min(w, t) for w in WINDOWS]
    kv_rows = [min(k, TM) for k in keep]
    first = [(t - k) // TM for k in keep]
    seqs, steps_s, _ = qkv_s.shape
    nb = seqs // (n * steps)
    assert nb * n * steps == seqs
    views = _cache_views(caches, steps_s)

    def kv_spec(g):
        return pl.BlockSpec((None, kv_rows[g], 2 * GROUP_W),
                            lambda i, j, g=g: (i, jnp.maximum(j - first[g], 0), 0))

    def qkv_spec(g):
        return pl.BlockSpec((None, DILS[g], TM // DILS[g], QKV_G), lambda i, j: (i, 0, j, 0))

    tok = lambda w: pl.BlockSpec((None, TM, w), lambda i, j: (i, j, 0))
    return pl.pallas_call(
        _mix_in_prompt_kernel,
        grid=(n, steps),
        in_specs=[
            tok(D),
            pl.BlockSpec((None, 1, 6 * D), lambda i, j: (i, 0, 0)),
            _const_spec((None, 4, D), lambda i, j: (l, 0, 0)),
            _const_spec((None, D, IN_W), lambda i, j: (l, 0, 0)),
            _const_spec((None, 1, A_W), lambda i, j: (l, 0, 0)),
            _const_spec((None, 1, A_W), lambda i, j: (l, 0, 0)),
            _const_spec((None, 4, CHUNK, CHUNK), lambda i, j: (l, 0, 0, 0)),
            _const_spec((None, CHUNK, A_W), lambda i, j: (l, 0, 0)),
            _const_spec((None, A_W, D), lambda i, j: (l, 0, 0)),
            pl.BlockSpec((nb, steps_s, QKV_W), lambda i, j: (i * steps + j, 0, 0)),
        ] + [
            pl.BlockSpec((None, nb, 2 * GROUP_W, v.shape[3]), lambda i, j: (l, i * steps + j, 0, 0))
            for v in views
        ],
        out_specs=[tok(D), tok(D), qkv_spec(0), qkv_spec(1), qkv_spec(2),
                   kv_spec(0), kv_spec(1), kv_spec(2),
                   pl.BlockSpec((nb, steps_s, GROUP_W), lambda i, j: (i * steps + j, 0, 0))],
        out_shape=[
            jax.ShapeDtypeStruct((n, t, D), BF16),
            jax.ShapeDtypeStruct((n, t, D), BF16),
        ] + [jax.ShapeDtypeStruct((n, d, t // d, QKV_G), BF16) for d in DILS] + [
            jax.ShapeDtypeStruct((n, k, 2 * GROUP_W), F32) for k in keep
        ] + [jax.ShapeDtypeStruct((seqs, steps_s, GROUP_W), F32)],
        scratch_shapes=[pltpu.VMEM((TM, A_W), BF16), pltpu.VMEM((QKV_G // LANES, TM, LANES), F32)],
        compiler_params=_params("arbitrary", "arbitrary"),
        name="mix_in_prompt",
    )(x, mod, norm_g, w_in, ln_g, ln_b, w_sp, b_sp, w_a2d, qkv_s, *views)


def _mix_in_sample_kernel(x_ref, mod_ref, ng_ref, win_ref, lng_ref, lnb_ref, csp_ref, bsp_ref, wa_ref,
                          za_ref, gb_ref, qkv_ref, vn_ref, *, steps):
    tm = x_ref.shape[0]
    h, gu, vn = _project_in(x_ref[...], mod_ref[:, 0:D], mod_ref[:, D:2 * D],
                            ng_ref, win_ref, lng_ref, lnb_ref)
    vn_ref[...] = vn
    mix = jnp.zeros((tm // 8, 8, A_W), F32) + bsp_ref[...][None]
    for k in range(steps):
        prev = vn if k == 0 else pltpu.roll(vn, k, 0)
        mix = mix + csp_ref[k][None] * prev.reshape(tm // 8, 8, A_W)
    ya = (gu * mix.reshape(tm, A_W)).astype(BF16)
    _gates_out(h, ya, win_ref, wa_ref, za_ref, gb_ref)
    qkv_ref[...] = _dot(h, win_ref[:, O_QKV:O_GATE])


def _mix_in_sample(l, x, mod, norm_g, w_in, ln_g, ln_b, c_sp, b_sp, w_a2d, steps):
    rows = x.shape[0]
    tm = TM_SAMPLE
    tok = lambda w: pl.BlockSpec((tm, w), lambda i: (i, 0))
    return pl.pallas_call(
        functools.partial(_mix_in_sample_kernel, steps=steps),
        grid=(rows // tm,),
        in_specs=[
            tok(D),
            pl.BlockSpec((None, tm, 2 * D), lambda i: (l, i, 0)),
            _const_spec((None, 4, D), lambda i: (l, 0, 0)),
            _const_spec((None, D, IN_W), lambda i: (l, 0, 0)),
            _const_spec((None, 1, A_W), lambda i: (l, 0, 0)),
            _const_spec((None, 1, A_W), lambda i: (l, 0, 0)),
            _const_spec((None, steps, 8, A_W), lambda i: (l, 0, 0, 0)),
            _const_spec((None, 8, A_W), lambda i: (l, 0, 0)),
            _const_spec((None, A_W, D), lambda i: (l, 0, 0)),
        ],
        out_specs=[tok(D), tok(D), tok(QKV_W), tok(A_W)],
        out_shape=[
            jax.ShapeDtypeStruct((rows, D), BF16),
            jax.ShapeDtypeStruct((rows, D), BF16),
            jax.ShapeDtypeStruct((rows, QKV_W), F32),
            jax.ShapeDtypeStruct((rows, A_W), F32),
        ],
        compiler_params=_params("arbitrary"),
        name="mix_in_sample",
    )(x, mod, norm_g, w_in, ln_g, ln_b, c_sp, b_sp, w_a2d)


def _band_attn_kernel(qkv_ref, o_ref, lse_ref, *, span):
    n_res, length, _ = qkv_ref.shape
    nb = length // CHUNK
    kw = CHUNK if nb == 1 else 2 * CHUNK
    head = _head_of_lane()
    qi = lax.broadcasted_iota(jnp.int32, (CHUNK, kw), 0)
    ki = lax.broadcasted_iota(jnp.int32, (CHUNK, kw), 1)
    ones = jnp.ones((kw, LANES), BF16)

    def unit(u, carry):
        r = u // nb
        b = u % nb
        q0 = pl.multiple_of(b * CHUNK, CHUNK)
        k0 = pl.multiple_of(jnp.maximum(b - 1, 0) * CHUNK, CHUNK)
        q = qkv_ref[r, pl.ds(q0, CHUNK), 0:GROUP_W]
        k = qkv_ref[r, pl.ds(k0, kw), GROUP_W:2 * GROUP_W]
        v = qkv_ref[r, pl.ds(k0, kw), 2 * GROUP_W:QKV_G]
        dist = (q0 - k0) + qi - ki
        valid = (dist >= 0) & (dist <= span)
        qs = jnp.concatenate([jnp.where(head == hh, q, jnp.zeros_like(q)) for hh in range(HEADS)], axis=0)
        s = _dot_nt(qs, k).reshape(HEADS, CHUNK, kw)
        s = jnp.where(valid[None], s, NEG).reshape(HEADS * CHUNK, kw)
        m = jnp.max(s, axis=-1, keepdims=True)
        p = jnp.exp2(s - m).astype(BF16)
        z = _dot(p, ones)
        z = jnp.concatenate([z, z], axis=-1)
        o = _dot(p, v)
        acc, zacc = o[0:CHUNK], z[0:CHUNK]
        macc = jnp.broadcast_to(m[0:CHUNK], (CHUNK, GROUP_W))
        for hh in range(1, HEADS):
            rows = slice(hh * CHUNK, (hh + 1) * CHUNK)
            acc = jnp.where(head == hh, o[rows], acc)
            zacc = jnp.where(head == hh, z[rows], zacc)
            macc = jnp.where(head == hh, m[rows], macc)
        o_ref[r, pl.ds(q0, CHUNK), :] = (acc / zacc).astype(o_ref.dtype)
        lse_ref[r, pl.ds(q0, CHUNK), :] = (macc + jnp.log2(zacc)) * LN_2
        return carry

    lax.fori_loop(0, n_res * nb, unit, 0, unroll=8)


def _band_attn(qkv, g):
    n, dil, length, _ = qkv.shape
    span = WINDOWS[g] // dil
    out_spec = pl.BlockSpec((None, dil, length, GROUP_W), lambda i: (i, 0, 0, 0))
    return pl.pallas_call(
        functools.partial(_band_attn_kernel, span=span),
        grid=(n,),
        in_specs=[pl.BlockSpec((None, dil, length, QKV_G), lambda i: (i, 0, 0, 0))],
        out_specs=[out_spec, out_spec],
        out_shape=[
            jax.ShapeDtypeStruct((n, dil, length, GROUP_W), BF16),
            jax.ShapeDtypeStruct((n, dil, length, GROUP_W), F32),
        ],
        compiler_params=_params("arbitrary"),
        name=f"band_attn_g{g}",
    )(qkv)


def _cached_group(q, k_new, v_new, cache_ref, dil):
    steps = q.shape[0]
    past = cache_ref.shape[1]
    head = _head_of_lane()
    q8 = jnp.concatenate([q * Q_SCALE, jnp.zeros((8 - steps, GROUP_W), F32)], axis=0).astype(BF16)
    qs = jnp.concatenate([jnp.where(head == hh, q8, jnp.zeros_like(q8)) for hh in range(HEADS)], axis=0)
    kt = cache_ref[0:GROUP_W, :].astype(BF16)
    vt = cache_ref[GROUP_W:2 * GROUP_W, :].astype(BF16)
    rows = HEADS * 8
    step_c = lax.broadcasted_iota(jnp.int32, (rows, past), 0) % steps
    pos = lax.broadcasted_iota(jnp.int32, (rows, past), 1)
    step_n = lax.broadcasted_iota(jnp.int32, (rows, steps), 0) % steps
    new = lax.broadcasted_iota(jnp.int32, (rows, steps), 1)
    if dil == 1:
        valid_c = pos >= step_c
        valid_n = new <= step_n
    else:
        valid_c = pos % dil == step_c
        valid_n = new == step_n
    s_c = jnp.where(valid_c, _dot(qs, kt), NEG)
    s_n = jnp.where(valid_n, _dot_nt(qs, k_new.astype(BF16)), NEG)
    m = jnp.maximum(jnp.max(s_c, axis=-1, keepdims=True), jnp.max(s_n, axis=-1, keepdims=True))
    p_c = jnp.exp(s_c - m)
    p_n = jnp.exp(s_n - m)
    z = jnp.sum(p_c, axis=-1, keepdims=True) + jnp.sum(p_n, axis=-1, keepdims=True)
    o = (_dot_nt(p_c.astype(BF16), vt) + _dot(p_n.astype(BF16), v_new.astype(BF16))) / z
    lse = jnp.broadcast_to(m + jnp.log(z), (rows, GROUP_W))
    acc = o[0:8]
    lacc = lse[0:8]
    for hh in range(1, HEADS):
        acc = jnp.where(head == hh, o[hh * 8:(hh + 1) * 8], acc)
        lacc = jnp.where(head == hh, lse[hh * 8:(hh + 1) * 8], lacc)
    return acc, lacc


def _cache_attn_body(qkv_ref, cache_refs, y_ref):
    nseq, steps, _ = qkv_ref.shape
    for i in range(nseq):
        outs = []
        for g, c_ref in enumerate(cache_refs):
            q, k_new, v_new = (qkv_ref[i, :, _qkv_col(which, g):_qkv_col(which, g) + GROUP_W]
                               for which in range(3))
            outs.append(_cached_group(q, k_new, v_new, c_ref.at[i], DILS[g]))
        y_ref[i] = _merge(outs)[0:steps]


def _merge(groups):
    lmax = groups[0][1]
    for _, ls in groups[1:]:
        lmax = jnp.maximum(lmax, ls)
    num = 0.0
    den = 0.0
    for o, ls in groups:
        e = jnp.exp(ls - lmax)
        num = num + e * o
        den = den + e
    return num / den


def _cache_views(caches, steps):
    views = []
    for g, c in enumerate(caches):
        depth, seqs, past = c.shape[:3]
        assert past == WINDOWS[g] and steps <= DILS[1], "window fully cached; steps fit one residue period"
        views.append(c.transpose(0, 1, 3, 4, 5, 2).reshape(depth, seqs, 2 * GROUP_W, past))
    return views


def _mix_out_front(x, yb, za, gb, mods, ng_ref, wb_ref, wo_ref):
    sh2, sc2, gt1 = mods
    merged = za.astype(F32) + gb.astype(F32) * _dot(yb, wb_ref[...])
    x1 = x + gt1 * _rms(_dot(merged.astype(BF16), wo_ref[...]), ng_ref[1:2, :])
    h2 = (_rms(x1, ng_ref[2:3, :]) * (1.0 + sc2) + sh2).astype(BF16)
    return x1, h2


def _conv_glu(h2, earlier_rows, wup_ref, cw_ref, cb_ref, wdn_ref):
    f = None
    for c in range(D_FF // FF_CHUNK):
        cols = slice(c * FF_CHUNK, (c + 1) * FF_CHUNK)
        gate = _dot(h2, wup_ref[:, cols])
        val = _dot(h2, wup_ref[:, D_FF + c * FF_CHUNK:D_FF + (c + 1) * FF_CHUNK])
        g1, g2 = earlier_rows(cols, gate)
        conv = cb_ref[:, cols] + cw_ref[0:1, cols] * g2 + cw_ref[1:2, cols] * g1 + cw_ref[2:3, cols] * gate
        part = _dot((_gelu(conv) * val).astype(BF16), wdn_ref[cols, :])
        f = part if f is None else f + part
    return f


def _interleave(blk_ref, scr, base, tm):
    dil = blk_ref.shape[0]
    if dil == 1:
        return blk_ref[0].astype(F32)
    for r in range(dil):
        x = blk_ref[r].astype(F32)
        for s in range(GROUP_W // LANES):
            scr[base + s, pl.ds(r, tm // dil, stride=dil), :] = x[:, s * LANES:(s + 1) * LANES]
    return jnp.concatenate([scr[base + s] for s in range(GROUP_W // LANES)], axis=-1)


def _mix_out_prompt_kernel(x_ref, za_ref, gb_ref, o0_ref, o1_ref, o2_ref, l0_ref, l1_ref, l2_ref,
                           mod_ref, ng_ref, wb_ref, wo_ref, wup_ref, cw_ref, cb_ref, wdn_ref,
                           y_ref, cs_ref, tail_scr, perm_scr):
    tm = x_ref.shape[0]

    @pl.when(pl.program_id(1) == 0)
    def _():
        tail_scr[...] = jnp.zeros_like(tail_scr)

    slabs = GROUP_W // LANES
    groups = []
    for g, (o_ref, l_ref) in enumerate(((o0_ref, l0_ref), (o1_ref, l1_ref), (o2_ref, l2_ref))):
        groups.append((_interleave(o_ref, perm_scr, 2 * g * slabs, tm),
                       _interleave(l_ref, perm_scr, (2 * g + 1) * slabs, tm)))
    yb = _merge(groups).astype(BF16)
    mods = (mod_ref[:, 3 * D:4 * D], mod_ref[:, 4 * D:5 * D], mod_ref[:, 2 * D:3 * D])
    row = lax.broadcasted_iota(jnp.int32, (SUB_OUT, FF_CHUNK), 0)
    tails = {}
    n_sub = tm // SUB_OUT
    for h in range(n_sub):
        rows = slice(h * SUB_OUT, (h + 1) * SUB_OUT)
        x1, h2 = _mix_out_front(x_ref[rows, :], yb[rows, :], za_ref[rows, :], gb_ref[rows, :], mods,
                                ng_ref, wb_ref, wo_ref)

        def earlier_rows(cols, gate, first=(h == 0), last=(h == n_sub - 1)):
            prev = tail_scr[:, cols] if first else tails[cols.start]
            g1 = jnp.where(row == 0, prev[7:8, :], pltpu.roll(gate, 1, 0))
            g2 = jnp.where(row == 0, prev[6:7, :],
                           jnp.where(row == 1, prev[7:8, :], pltpu.roll(gate, 2, 0)))
            tails[cols.start] = gate[SUB_OUT - 8:, :]
            if last:
                tail_scr[:, cols] = gate[SUB_OUT - 8:, :]
                cs_ref[:, cols] = gate[SUB_OUT - (CONV_W - 1):, :]
            return g1, g2

        f = _conv_glu(h2, earlier_rows, wup_ref, cw_ref, cb_ref, wdn_ref)
        y_ref[rows, :] = x1 + mod_ref[:, 5 * D:6 * D] * _rms(f, ng_ref[3:4, :])


def _mix_out_prompt(l, x, za, gb, os, lses, mod, norm_g, w_b2d, w_out, w_up, conv_w, conv_b, w_down):
    n, t, _ = x.shape
    tm = TM_OUT
    tok = lambda w: pl.BlockSpec((None, tm, w), lambda i, j: (i, j, 0))
    res = lambda d: pl.BlockSpec((None, d, tm // d, GROUP_W), lambda i, j: (i, 0, j, 0))
    return pl.pallas_call(
        _mix_out_prompt_kernel,
        grid=(n, t // tm),
        in_specs=[
            tok(D), tok(D), tok(D),
            res(DILS[0]), res(DILS[1]), res(DILS[2]), res(DILS[0]), res(DILS[1]), res(DILS[2]),
            pl.BlockSpec((None, 1, 6 * D), lambda i, j: (i, 0, 0)),
            _const_spec((None, 4, D), lambda i, j: (l, 0, 0)),
            _const_spec((None, GROUP_W, D), lambda i, j: (l, 0, 0)),
            _const_spec((None, D, D), lambda i, j: (l, 0, 0)),
            _const_spec((None, D, 2 * D_FF), lambda i, j: (l, 0, 0)),
            _const_spec((None, CONV_W, D_FF), lambda i, j: (l, 0, 0)),
            _const_spec((None, 1, D_FF), lambda i, j: (l, 0, 0)),
            _const_spec((None, D_FF, D), lambda i, j: (l, 0, 0)),
        ],
        out_specs=[tok(D), pl.BlockSpec((None, CONV_W - 1, D_FF), lambda i, j: (i, 0, 0))],
        out_shape=[
            jax.ShapeDtypeStruct((n, t, D), F32),
            jax.ShapeDtypeStruct((n, CONV_W - 1, D_FF), F32),
        ],
        scratch_shapes=[pltpu.VMEM((8, D_FF), F32),
                        pltpu.VMEM((2 * N_GROUPS * GROUP_W // LANES, tm, LANES), F32)],
        compiler_params=_params("arbitrary", "arbitrary"),
        name="mix_out_prompt",
    )(x, za, gb, *os, *lses, mod, norm_g, w_b2d, w_out, w_up, conv_w, conv_b, w_down)


def _mix_out_sample_kernel(x_ref, za_ref, gb_ref, yb_ref, moda_ref, modb_ref, st_ref,
                           ng_ref, wb_ref, wo_ref, wup_ref, cw_ref, cb_ref, wdn_ref,
                           y_ref, gate_ref, fill_scr, *, steps):
    tm = x_ref.shape[0]
    nseq = tm // steps
    slabs = D_FF // LANES
    fill_scr[...] = jnp.zeros_like(fill_scr)
    for s in range(slabs):
        older = st_ref[s, pl.ds(0, nseq, stride=CONV_W - 1), :]
        newer = st_ref[s, pl.ds(1, nseq, stride=CONV_W - 1), :]
        fill_scr[s, pl.ds(0, nseq, stride=steps), :] = newer
        fill_scr[slabs + s, pl.ds(0, nseq, stride=steps), :] = older
        fill_scr[slabs + s, pl.ds(1, nseq, stride=steps), :] = newer

    mods = (moda_ref[:, D:2 * D], modb_ref[:, 0:D], moda_ref[:, 0:D])
    x1, h2 = _mix_out_front(x_ref[...], yb_ref[...].astype(BF16), za_ref[...], gb_ref[...], mods,
                            ng_ref, wb_ref, wo_ref)
    step = lax.broadcasted_iota(jnp.int32, (tm, FF_CHUNK), 0) % steps

    def earlier_rows(cols, gate):
        gate_ref[:, cols] = gate
        first = cols.start // LANES
        fill = lambda base: jnp.concatenate(
            [fill_scr[base + first + s] for s in range(FF_CHUNK // LANES)], axis=-1)
        g1 = jnp.where(step >= 1, pltpu.roll(gate, 1, 0), fill(0))
        g2 = jnp.where(step >= 2, pltpu.roll(gate, 2, 0), fill(slabs))
        return g1, g2

    f = _conv_glu(h2, earlier_rows, wup_ref, cw_ref, cb_ref, wdn_ref)
    y_ref[...] = x1 + modb_ref[:, D:2 * D] * _rms(f, ng_ref[3:4, :])


def _mix_out_sample(l, x, za, gb, yb, mod, state, norm_g, w_b2d, w_out, w_up, conv_w, conv_b,
                    w_down, steps):
    rows = x.shape[0]
    tm = TM_SAMPLE
    slabs = D_FF // LANES
    st_rows = tm // steps * (CONV_W - 1)
    tok = lambda w: pl.BlockSpec((tm, w), lambda i: (i, 0))
    return pl.pallas_call(
        functools.partial(_mix_out_sample_kernel, steps=steps),
        grid=(rows // tm,),
        in_specs=[
            tok(D), tok(D), tok(D), tok(GROUP_W),
            pl.BlockSpec((None, tm, 2 * D), lambda i: (l, i, 1)),
            pl.BlockSpec((None, tm, 2 * D), lambda i: (l, i, 2)),
            pl.BlockSpec((None, slabs, st_rows, LANES), lambda i: (l, 0, i, 0)),
            _const_spec((None, 4, D), lambda i: (l, 0, 0)),
            _const_spec((None, GROUP_W, D), lambda i: (l, 0, 0)),
            _const_spec((None, D, D), lambda i: (l, 0, 0)),
            _const_spec((None, D, 2 * D_FF), lambda i: (l, 0, 0)),
            _const_spec((None, CONV_W, D_FF), lambda i: (l, 0, 0)),
            _const_spec((None, 1, D_FF), lambda i: (l, 0, 0)),
            _const_spec((None, D_FF, D), lambda i: (l, 0, 0)),
        ],
        out_specs=[tok(D), tok(D_FF)],
        out_shape=[
            jax.ShapeDtypeStruct((rows, D), F32),
            jax.ShapeDtypeStruct((rows, D_FF), F32),
        ],
        scratch_shapes=[pltpu.VMEM((2 * slabs, tm, LANES), F32)],
        compiler_params=_params("arbitrary"),
        name="mix_out_sample",
    )(x, za, gb, yb, mod, mod, state, norm_g, w_b2d, w_out, w_up, conv_w, conv_b, w_down)


def _sample_spatial_tables(w_spatial, b_spatial, steps):
    step = jnp.arange(8) % steps
    w_first = w_spatial[:, :, :steps, :steps]
    coefs = []
    for k in range(steps):
        src = step - k
        w = w_first[:, :, step, jnp.maximum(src, 0)]
        coefs.append(jnp.where(src >= 0, w, 0.0))
    c_sp = jnp.stack(coefs, axis=1).transpose(0, 1, 3, 2)
    b_sp = b_spatial[:, :, step].transpose(0, 2, 1)
    return jnp.repeat(c_sp, CHUNK, axis=-1), jnp.repeat(b_sp, CHUNK, axis=-1)


def kernel(x_prompt, x_sample, cache_swa0, cache_swa1, cache_swa2, state_ffn_conv, c_prompt, c_sample,
           ada_w, ada_b, norm_g, w_in, ln_v_g, ln_v_b, w_spatial, b_spatial, w_a2d, w_b2d, w_out,
           w_up, conv_w, conv_b, w_down):
    n_p, t_p, _ = x_prompt.shape
    seqs, steps, _ = x_sample.shape
    rows_s = seqs * steps
    caches = (cache_swa0, cache_swa1, cache_swa2)

    w_in_b = w_in.astype(BF16)
    w_a2d_b, w_b2d_b, w_out_b = w_a2d.astype(BF16), w_b2d.astype(BF16), w_out.astype(BF16)
    w_up_b, w_down_b = w_up.astype(BF16), w_down.astype(BF16)
    ln_g3 = ln_v_g.reshape(DEPTH, 1, A_W)
    ln_b3 = ln_v_b.reshape(DEPTH, 1, A_W)
    conv_b3 = conv_b.reshape(DEPTH, 1, D_FF)
    b_sp = jnp.repeat(b_spatial.transpose(0, 2, 1), CHUNK, axis=-1)
    c_sp_s, b_sp_s = _sample_spatial_tables(w_spatial, b_spatial, steps)

    c_rows = jnp.concatenate([jnp.repeat(c_sample, steps, axis=0), c_prompt], axis=0)
    mod = _modulation(c_rows, ada_w, ada_b)
    mod_p = mod[:, rows_s:].reshape(DEPTH, n_p, 1, 6 * D)

    state_slabs = state_ffn_conv.reshape(DEPTH, seqs * (CONV_W - 1), D_FF // LANES, LANES)
    state_slabs = state_slabs.transpose(0, 2, 1, 3)

    y_p = x_prompt
    y_s = x_sample.reshape(rows_s, D)

    swa_p = [[] for _ in range(N_GROUPS)]
    swa_s = [[] for _ in range(N_GROUPS)]
    conv_p, conv_s, chunk_v_s = [], [], []
    for l in range(DEPTH):
        za_s, gb_s, qkv_s, vn_s = _mix_in_sample(
            l, y_s, mod, norm_g, w_in_b, ln_g3, ln_b3, c_sp_s, b_sp_s, w_a2d_b, steps)

        za, gb, qkv0, qkv1, qkv2, kv0, kv1, kv2, yb_s = _mix_in_prompt(
            l, y_p, mod_p[l], norm_g, w_in_b, ln_g3, ln_b3, w_spatial, b_sp, w_a2d_b,
            qkv_s.reshape(seqs, steps, QKV_W), caches)
        attn = [_band_attn(q, g) for g, q in enumerate((qkv0, qkv1, qkv2))]
        y_p, cst_p = _mix_out_prompt(
            l, y_p, za, gb, [a[0] for a in attn], [a[1] for a in attn], mod_p[l], norm_g,
            w_b2d_b, w_out_b, w_up_b, conv_w, conv_b3, w_down_b)
        for g, kv in enumerate((kv0, kv1, kv2)):
            swa_p[g].append(kv.reshape(n_p, kv.shape[1], 2, HEADS, HEAD_DIM))
        conv_p.append(cst_p)

        y_s, gate_s = _mix_out_sample(
            l, y_s, za_s, gb_s, yb_s.reshape(rows_s, GROUP_W), mod, state_slabs, norm_g,
            w_b2d_b, w_out_b, w_up_b, conv_w, conv_b3, w_down_b, steps)
        qkv_s6 = qkv_s.reshape(seqs, steps, 3, N_GROUPS, HEADS, HEAD_DIM)
        for g in range(N_GROUPS):
            swa_s[g].append(qkv_s6[:, :, 1:3, g])
        conv_s.append(gate_s.reshape(seqs, steps, D_FF)[:, steps - (CONV_W - 1):])
        chunk_v_s.append(vn_s.reshape(seqs, steps, A_W))

    return (y_p, y_s.reshape(seqs, steps, D),
            jnp.stack(swa_p[0]), jnp.stack(swa_p[1]), jnp.stack(swa_p[2]), jnp.stack(conv_p),
            jnp.stack(swa_s[0]), jnp.stack(swa_s[1]), jnp.stack(swa_s[2]), jnp.stack(conv_s),
            jnp.stack(chunk_v_s))
```

```python
import functools

import jax
import jax.numpy as jnp
from jax import lax
from jax.experimental import pallas as pl
from jax.experimental.pallas import tpu as pltpu

D = 1024
DEPTH = 4
CHUNK = 128
LANES = 128
A_W = 512
N_GROUPS = 3
DILS = (1, 4, 16)
WINDOWS = (128, 512, 2048)
GROUP_W = 256
HEAD_DIM = 64
HEADS = GROUP_W // HEAD_DIM
QKV_G = 3 * GROUP_W
QKV_W = N_GROUPS * QKV_G
D_FF = 2816
IN_W = 2 * A_W + QKV_W + 2 * D
O_QKV = 2 * A_W
O_GATE = O_QKV + QKV_W
CONV_W = 3
EPS = 1e-6
NEG = -1e30
Q_SCALE = HEAD_DIM ** -0.5
LOG2_E = 1.4426950408889634
LN_2 = 0.6931471805599453

TM = 256
TM_OUT = 512
SUB_OUT = 256
COL_CHUNK = 256
FF_CHUNK = D_FF
TM_SAMPLE = 128
VMEM_LIMIT = 56 * 1024 * 1024

F32 = jnp.float32
BF16 = jnp.bfloat16
_NT = (((1,), (1,)), ((), ()))


def _dot(a, b):
    return jnp.dot(a, b, preferred_element_type=F32)


def _dot_nt(a, b):
    return lax.dot_general(a, b, _NT, preferred_element_type=F32)


def _rms(x, g):
    return x * lax.rsqrt(jnp.mean(x * x, axis=-1, keepdims=True) + EPS) * g


def _gelu(x):
    return jax.nn.gelu(x)


def _layer_norm(x, g, b):
    mu = jnp.mean(x, axis=-1, keepdims=True)
    xc = x - mu
    var = jnp.mean(xc * xc, axis=-1, keepdims=True)
    return xc * lax.rsqrt(var + EPS) * g + b


def _const_spec(shape, index_map):
    return pl.BlockSpec(shape, index_map, pipeline_mode=pl.Buffered(1))


def _params(*sem):
    return pltpu.CompilerParams(dimension_semantics=sem, vmem_limit_bytes=VMEM_LIMIT)


def _head_of_lane():
    return lax.broadcasted_iota(jnp.int32, (1, GROUP_W), 1) // HEAD_DIM


def _mod_kernel(c_ref, w_ref, b_ref, o_ref):
    c = c_ref[...]
    s = (c * jax.nn.sigmoid(c)).astype(BF16)
    o_ref[...] = _dot(s, w_ref[...].astype(BF16)) + b_ref[...]


def _modulation(c_all, ada_w, ada_b):
    rows = c_all.shape[0]
    tn = 1536
    return pl.pallas_call(
        _mod_kernel,
        grid=(DEPTH, 6 * D // tn),
        in_specs=[
            pl.BlockSpec((rows, D), lambda l, j: (0, 0)),
            pl.BlockSpec((None, D, tn), lambda l, j: (l, 0, j)),
            pl.BlockSpec((None, 1, tn), lambda l, j: (l, 0, j)),
        ],
        out_specs=pl.BlockSpec((None, rows, tn), lambda l, j: (l, 0, j)),
        out_shape=jax.ShapeDtypeStruct((DEPTH, rows, 6 * D), F32),
        compiler_params=_params("arbitrary", "arbitrary"),
        name="adaln_mod",
    )(c_all, ada_w, ada_b.reshape(DEPTH, 1, 6 * D))


def _project_in(x, sh1, sc1, ng_ref, win_ref, lng_ref, lnb_ref):
    h = (_rms(x, ng_ref[0:1, :]) * (1.0 + sc1) + sh1).astype(BF16)
    puv = _dot(h, win_ref[:, 0:O_QKV])
    gu = _gelu(puv[:, :A_W])
    vn = _layer_norm(_gelu(puv[:, A_W:]), lng_ref[...], lnb_ref[...])
    return h, gu, vn


def _gates_out(h, ya, win_ref, wa_ref, za_ref, gb_ref):
    for c in range(D // COL_CHUNK):
        cols = slice(c * COL_CHUNK, (c + 1) * COL_CHUNK)
        ga = jax.nn.sigmoid(_dot(h, win_ref[:, O_GATE + c * COL_CHUNK:O_GATE + (c + 1) * COL_CHUNK]))
        za_ref[:, cols] = (ga * _dot(ya, wa_ref[:, cols])).astype(za_ref.dtype)
        o = O_GATE + D + c * COL_CHUNK
        gb_ref[:, cols] = jax.nn.sigmoid(_dot(h, win_ref[:, o:o + COL_CHUNK])).astype(gb_ref.dtype)


def _qkv_col(which, g):
    return (which * N_GROUPS + g) * GROUP_W


def _mix_in_prompt_kernel(x_ref, mod_ref, ng_ref, win_ref, lng_ref, lnb_ref, ws_ref, bsp_ref, wa_ref,
                          qkvs_ref, c0_ref, c1_ref, c2_ref,
                          za_ref, gb_ref, qkv0_ref, qkv1_ref, qkv2_ref, kv0_ref, kv1_ref, kv2_ref, ys_ref,
                          ya_scr, perm_scr):
    _cache_attn_body(qkvs_ref, (c0_ref, c1_ref, c2_ref), ys_ref)
    tm = x_ref.shape[0]
    h, gu, vn = _project_in(x_ref[...], mod_ref[:, 0:D], mod_ref[:, D:2 * D],
                            ng_ref, win_ref, lng_ref, lnb_ref)
    vnb = vn.astype(BF16)
    ti = lax.broadcasted_iota(jnp.int32, (CHUNK, CHUNK), 0)
    si = lax.broadcasted_iota(jnp.int32, (CHUNK, CHUNK), 1)
    causal = si <= ti
    for g in range(A_W // CHUNK):
        wg = jnp.where(causal, ws_ref[g], 0.0).astype(BF16)
        for c in range(tm // CHUNK):
            rows = slice(c * CHUNK, (c + 1) * CHUNK)
            cols = slice(g * CHUNK, (g + 1) * CHUNK)
            mix = _dot(wg, vnb[rows, cols]) + bsp_ref[:, cols]
            ya_scr[rows, cols] = (gu[rows, cols] * mix).astype(BF16)
    _gates_out(h, ya_scr[...], win_ref, wa_ref, za_ref, gb_ref)

    pq = _dot(h, win_ref[:, O_QKV:O_GATE])
    for g, kv_ref in enumerate((kv0_ref, kv1_ref, kv2_ref)):
        first = tm - kv_ref.shape[0]
        for which in (1, 2):
            col = _qkv_col(which, g)
            kv_ref[:, (which - 1) * GROUP_W:which * GROUP_W] = pq[first:, col:col + GROUP_W]
    halves = GROUP_W // LANES
    for g, out_ref in enumerate((qkv0_ref, qkv1_ref, qkv2_ref)):
        dil = DILS[g]
        for which in range(3):
            scale = Q_SCALE * LOG2_E if which == 0 else 1.0
            col = _qkv_col(which, g)
            if dil == 1:
                out_ref[0, :, which * GROUP_W:(which + 1) * GROUP_W] = (
                    pq[:, col:col + GROUP_W] * scale).astype(BF16)
                continue
            for s in range(halves):
                perm_scr[which * halves + s] = pq[:, col + s * LANES:col + (s + 1) * LANES] * scale
        if dil == 1:
            continue
        for r in range(dil):
            for s in range(3 * halves):
                out_ref[r, :, s * LANES:(s + 1) * LANES] = (
                    perm_scr[s, pl.ds(r, tm // dil, stride=dil), :].astype(BF16))


def _mix_in_prompt(l, x, mod, norm_g, w_in, ln_g, ln_b, w_sp, b_sp, w_a2d, qkv_s, caches):
    n, t, _ = x.shape
    steps = t // TM
    keep = [min(w, t) for w in WINDOWS]
    kv_rows = [min(k, TM) for k in keep]
    first = [(t - k) // TM for k in keep]
    seqs = caches[0].shape[1]
    steps_s = qkv_s.shape[0] // seqs
    nb = seqs // (n * steps)
    rows_s = nb * steps_s
    assert nb * n * steps == seqs and rows_s % 8 == 0
    views = _cache_views(caches, steps_s)

    def kv_spec(g):
        return pl.BlockSpec((None, kv_rows[g], 2 * GROUP_W),
                            lambda i, j, g=g: (i, jnp.maximum(j - first[g], 0), 0))

    def qkv_spec(g):
        return pl.BlockSpec((None, DILS[g], TM // DILS[g], QKV_G), lambda i, j: (i, 0, j, 0))

    tok = lambda w: pl.BlockSpec((None, TM, w), lambda i, j: (i, j, 0))
    return pl.pallas_call(
        _mix_in_prompt_kernel,
        grid=(n, steps),
        in_specs=[
            tok(D),
            pl.BlockSpec((None, 1, 6 * D), lambda i, j: (i, 0, 0)),
            _const_spec((None, 4, D), lambda i, j: (l, 0, 0)),
            _const_spec((None, D, IN_W), lambda i, j: (l, 0, 0)),
            _const_spec((None, 1, A_W), lambda i, j: (l, 0, 0)),
            _const_spec((None, 1, A_W), lambda i, j: (l, 0, 0)),
            _const_spec((None, 4, CHUNK, CHUNK), lambda i, j: (l, 0, 0, 0)),
            _const_spec((None, CHUNK, A_W), lambda i, j: (l, 0, 0)),
            _const_spec((None, A_W, D), lambda i, j: (l, 0, 0)),
            pl.BlockSpec((rows_s, QKV_W), lambda i, j: (i * steps + j, 0)),
        ] + [
            pl.BlockSpec((None, nb, 2 * GROUP_W, v.shape[3]), lambda i, j: (l, i * steps + j, 0, 0))
            for v in views
        ],
        out_specs=[tok(D), tok(D), qkv_spec(0), qkv_spec(1), qkv_spec(2),
                   kv_spec(0), kv_spec(1), kv_spec(2),
                   pl.BlockSpec((rows_s, GROUP_W), lambda i, j: (i * steps + j, 0))],
        out_shape=[
            jax.ShapeDtypeStruct((n, t, D), BF16),
            jax.ShapeDtypeStruct((n, t, D), BF16),
        ] + [jax.ShapeDtypeStruct((n, d, t // d, QKV_G), BF16) for d in DILS] + [
            jax.ShapeDtypeStruct((n, k, 2 * GROUP_W), F32) for k in keep
        ] + [jax.ShapeDtypeStruct((seqs * steps_s, GROUP_W), F32)],
        scratch_shapes=[pltpu.VMEM((TM, A_W), BF16), pltpu.VMEM((QKV_G // LANES, TM, LANES), F32)],
        compiler_params=_params("arbitrary", "arbitrary"),
        name="mix_in_prompt",
    )(x, mod, norm_g, w_in, ln_g, ln_b, w_sp, b_sp, w_a2d, qkv_s, *views)


def _mix_in_sample_kernel(x_ref, mod_ref, ng_ref, win_ref, lng_ref, lnb_ref, csp_ref, bsp_ref, wa_ref,
                          za_ref, gb_ref, qkv_ref, vn_ref, *, steps):
    tm = x_ref.shape[0]
    h, gu, vn = _project_in(x_ref[...], mod_ref[:, 0:D], mod_ref[:, D:2 * D],
                            ng_ref, win_ref, lng_ref, lnb_ref)
    vn_ref[...] = vn
    mix = jnp.zeros((tm // 8, 8, A_W), F32) + bsp_ref[...][None]
    for k in range(steps):
        prev = vn if k == 0 else pltpu.roll(vn, k, 0)
        mix = mix + csp_ref[k][None] * prev.reshape(tm // 8, 8, A_W)
    ya = (gu * mix.reshape(tm, A_W)).astype(BF16)
    _gates_out(h, ya, win_ref, wa_ref, za_ref, gb_ref)
    qkv_ref[...] = _dot(h, win_ref[:, O_QKV:O_GATE])


def _mix_in_sample(l, x, mod, norm_g, w_in, ln_g, ln_b, c_sp, b_sp, w_a2d, steps):
    rows = x.shape[0]
    tm = TM_SAMPLE
    tok = lambda w: pl.BlockSpec((tm, w), lambda i: (i, 0))
    return pl.pallas_call(
        functools.partial(_mix_in_sample_kernel, steps=steps),
        grid=(rows // tm,),
        in_specs=[
            tok(D),
            pl.BlockSpec((None, tm, 2 * D), lambda i: (l, i, 0)),
            _const_spec((None, 4, D), lambda i: (l, 0, 0)),
            _const_spec((None, D, IN_W), lambda i: (l, 0, 0)),
            _const_spec((None, 1, A_W), lambda i: (l, 0, 0)),
            _const_spec((None, 1, A_W), lambda i: (l, 0, 0)),
            _const_spec((None, steps, 8, A_W), lambda i: (l, 0, 0, 0)),
            _const_spec((None, 8, A_W), lambda i: (l, 0, 0)),
            _const_spec((None, A_W, D), lambda i: (l, 0, 0)),
        ],
        out_specs=[tok(D), tok(D), tok(QKV_W), tok(A_W)],
        out_shape=[
            jax.ShapeDtypeStruct((rows, D), BF16),
            jax.ShapeDtypeStruct((rows, D), BF16),
            jax.ShapeDtypeStruct((rows, QKV_W), F32),
            jax.ShapeDtypeStruct((rows, A_W), F32),
        ],
        compiler_params=_params("arbitrary"),
        name="mix_in_sample",
    )(x, mod, norm_g, w_in, ln_g, ln_b, c_sp, b_sp, w_a2d)


def _band_attn_kernel(qkv_ref, o_ref, lse_ref, *, span):
    n_res, length, _ = qkv_ref.shape
    nb = length // CHUNK
    kw = CHUNK if nb == 1 else 2 * CHUNK
    head = _head_of_lane()
    qi = lax.broadcasted_iota(jnp.int32, (CHUNK, kw), 0)
    ki = lax.broadcasted_iota(jnp.int32, (CHUNK, kw), 1)
    ones = jnp.ones((kw, LANES), BF16)

    def unit(u, carry):
        r = u // nb
        b = u % nb
        q0 = pl.multiple_of(b * CHUNK, CHUNK)
        k0 = pl.multiple_of(jnp.maximum(b - 1, 0) * CHUNK, CHUNK)
        q = qkv_ref[r, pl.ds(q0, CHUNK), 0:GROUP_W]
        k = qkv_ref[r, pl.ds(k0, kw), GROUP_W:2 * GROUP_W]
        v = qkv_ref[r, pl.ds(k0, kw), 2 * GROUP_W:QKV_G]
        dist = (q0 - k0) + qi - ki
        valid = (dist >= 0) & (dist <= span)
        qs = jnp.concatenate([jnp.where(head == hh, q, jnp.zeros_like(q)) for hh in range(HEADS)], axis=0)
        s = _dot_nt(qs, k).reshape(HEADS, CHUNK, kw)
        s = jnp.where(valid[None], s, NEG).reshape(HEADS * CHUNK, kw)
        m = jnp.max(s, axis=-1, keepdims=True)
        p = jnp.exp2(s - m).astype(BF16)
        z = _dot(p, ones)
        z = jnp.concatenate([z, z], axis=-1)
        o = _dot(p, v)
        acc, zacc = o[0:CHUNK], z[0:CHUNK]
        macc = jnp.broadcast_to(m[0:CHUNK], (CHUNK, GROUP_W))
        for hh in range(1, HEADS):
            rows = slice(hh * CHUNK, (hh + 1) * CHUNK)
            acc = jnp.where(head == hh, o[rows], acc)
            zacc = jnp.where(head == hh, z[rows], zacc)
            macc = jnp.where(head == hh, m[rows], macc)
        o_ref[r, pl.ds(q0, CHUNK), :] = (acc / zacc).astype(o_ref.dtype)
        lse_ref[r, pl.ds(q0, CHUNK), :] = (macc + jnp.log2(zacc)) * LN_2
        return carry

    lax.fori_loop(0, n_res * nb, unit, 0, unroll=8)


def _band_attn(qkv, g):
    n, dil, length, _ = qkv.shape
    span = WINDOWS[g] // dil
    out_spec = pl.BlockSpec((None, dil, length, GROUP_W), lambda i: (i, 0, 0, 0))
    return pl.pallas_call(
        functools.partial(_band_attn_kernel, span=span),
        grid=(n,),
        in_specs=[pl.BlockSpec((None, dil, length, QKV_G), lambda i: (i, 0, 0, 0))],
        out_specs=[out_spec, out_spec],
        out_shape=[
            jax.ShapeDtypeStruct((n, dil, length, GROUP_W), BF16),
            jax.ShapeDtypeStruct((n, dil, length, GROUP_W), F32),
        ],
        compiler_params=_params("arbitrary"),
        name=f"band_attn_g{g}",
    )(qkv)


def _cached_group(q, k_new, v_new, cache_ref, dil):
    steps = q.shape[0]
    past = cache_ref.shape[1]
    head = _head_of_lane()
    q8 = jnp.concatenate([q * Q_SCALE, jnp.zeros((8 - steps, GROUP_W), F32)], axis=0).astype(BF16)
    qs = jnp.concatenate([jnp.where(head == hh, q8, jnp.zeros_like(q8)) for hh in range(HEADS)], axis=0)
    kt = cache_ref[0:GROUP_W, :].astype(BF16)
    vt = cache_ref[GROUP_W:2 * GROUP_W, :].astype(BF16)
    rows = HEADS * 8
    step_c = lax.broadcasted_iota(jnp.int32, (rows, past), 0) % steps
    pos = lax.broadcasted_iota(jnp.int32, (rows, past), 1)
    step_n = lax.broadcasted_iota(jnp.int32, (rows, steps), 0) % steps
    new = lax.broadcasted_iota(jnp.int32, (rows, steps), 1)
    if dil == 1:
        valid_c = pos >= step_c
        valid_n = new <= step_n
    else:
        valid_c = pos % dil == step_c
        valid_n = new == step_n
    s_c = jnp.where(valid_c, _dot(qs, kt), NEG)
    s_n = jnp.where(valid_n, _dot_nt(qs, k_new.astype(BF16)), NEG)
    m = jnp.maximum(jnp.max(s_c, axis=-1, keepdims=True), jnp.max(s_n, axis=-1, keepdims=True))
    p_c = jnp.exp(s_c - m)
    p_n = jnp.exp(s_n - m)
    z = jnp.sum(p_c, axis=-1, keepdims=True) + jnp.sum(p_n, axis=-1, keepdims=True)
    o = (_dot_nt(p_c.astype(BF16), vt) + _dot(p_n.astype(BF16), v_new.astype(BF16))) / z
    lse = jnp.broadcast_to(m + jnp.log(z), (rows, GROUP_W))
    acc = o[0:8]
    lacc = lse[0:8]
    for hh in range(1, HEADS):
        acc = jnp.where(head == hh, o[hh * 8:(hh + 1) * 8], acc)
        lacc = jnp.where(head == hh, lse[hh * 8:(hh + 1) * 8], lacc)
    return acc, lacc


def _cache_attn_body(qkv_ref, cache_refs, y_ref):
    nseq = cache_refs[0].shape[0]
    steps = qkv_ref.shape[0] // nseq
    for i in range(nseq):
        rows = slice(i * steps, (i + 1) * steps)
        outs = []
        for g, c_ref in enumerate(cache_refs):
            q, k_new, v_new = (qkv_ref[rows, _qkv_col(which, g):_qkv_col(which, g) + GROUP_W]
                               for which in range(3))
            outs.append(_cached_group(q, k_new, v_new, c_ref.at[i], DILS[g]))
        y_ref[rows, :] = _merge(outs)[0:steps]


def _merge(groups):
    lmax = groups[0][1]
    for _, ls in groups[1:]:
        lmax = jnp.maximum(lmax, ls)
    num = 0.0
    den = 0.0
    for o, ls in groups:
        e = jnp.exp(ls - lmax)
        num = num + e * o
        den = den + e
    return num / den


def _cache_views(caches, steps):
    views = []
    for g, c in enumerate(caches):
        depth, seqs, past = c.shape[:3]
        assert past == WINDOWS[g] and steps <= DILS[1], "window fully cached; steps fit one residue period"
        views.append(c.transpose(0, 1, 3, 4, 5, 2).reshape(depth, seqs, 2 * GROUP_W, past))
    return views


def _mix_out_front(x, yb, za, gb, mods, ng_ref, wb_ref, wo_ref):
    sh2, sc2, gt1 = mods
    merged = za.astype(F32) + gb.astype(F32) * _dot(yb, wb_ref[...])
    x1 = x + gt1 * _rms(_dot(merged.astype(BF16), wo_ref[...]), ng_ref[1:2, :])
    h2 = (_rms(x1, ng_ref[2:3, :]) * (1.0 + sc2) + sh2).astype(BF16)
    return x1, h2


def _conv_glu(h2, earlier_rows, wup_ref, cw_ref, cb_ref, wdn_ref):
    f = None
    for c in range(D_FF // FF_CHUNK):
        cols = slice(c * FF_CHUNK, (c + 1) * FF_CHUNK)
        gate = _dot(h2, wup_ref[:, cols])
        val = _dot(h2, wup_ref[:, D_FF + c * FF_CHUNK:D_FF + (c + 1) * FF_CHUNK])
        g1, g2 = earlier_rows(cols, gate)
        conv = cb_ref[:, cols] + cw_ref[0:1, cols] * g2 + cw_ref[1:2, cols] * g1 + cw_ref[2:3, cols] * gate
        part = _dot((_gelu(conv) * val).astype(BF16), wdn_ref[cols, :])
        f = part if f is None else f + part
    return f


def _interleave(blk_ref, scr, base, tm):
    dil = blk_ref.shape[0]
    if dil == 1:
        return blk_ref[0].astype(F32)
    for r in range(dil):
        x = blk_ref[r].astype(F32)
        for s in range(GROUP_W // LANES):
            scr[base + s, pl.ds(r, tm // dil, stride=dil), :] = x[:, s * LANES:(s + 1) * LANES]
    return jnp.concatenate([scr[base + s] for s in range(GROUP_W // LANES)], axis=-1)


def _mix_out_prompt_kernel(x_ref, za_ref, gb_ref, o0_ref, o1_ref, o2_ref, l0_ref, l1_ref, l2_ref,
                           mod_ref, ng_ref, wb_ref, wo_ref, wup_ref, cw_ref, cb_ref, wdn_ref,
                           y_ref, cs_ref, tail_scr, perm_scr):
    tm = x_ref.shape[0]

    @pl.when(pl.program_id(1) == 0)
    def _():
        tail_scr[...] = jnp.zeros_like(tail_scr)

    slabs = GROUP_W // LANES
    groups = []
    for g, (o_ref, l_ref) in enumerate(((o0_ref, l0_ref), (o1_ref, l1_ref), (o2_ref, l2_ref))):
        groups.append((_interleave(o_ref, perm_scr, 2 * g * slabs, tm),
                       _interleave(l_ref, perm_scr, (2 * g + 1) * slabs, tm)))
    yb = _merge(groups).astype(BF16)
    mods = (mod_ref[:, 3 * D:4 * D], mod_ref[:, 4 * D:5 * D], mod_ref[:, 2 * D:3 * D])
    row = lax.broadcasted_iota(jnp.int32, (SUB_OUT, FF_CHUNK), 0)
    tails = {}
    n_sub = tm // SUB_OUT
    for h in range(n_sub):
        rows = slice(h * SUB_OUT, (h + 1) * SUB_OUT)
        x1, h2 = _mix_out_front(x_ref[rows, :], yb[rows, :], za_ref[rows, :], gb_ref[rows, :], mods,
                                ng_ref, wb_ref, wo_ref)

        def earlier_rows(cols, gate, first=(h == 0), last=(h == n_sub - 1)):
            prev = tail_scr[:, cols] if first else tails[cols.start]
            g1 = jnp.where(row == 0, prev[7:8, :], pltpu.roll(gate, 1, 0))
            g2 = jnp.where(row == 0, prev[6:7, :],
                           jnp.where(row == 1, prev[7:8, :], pltpu.roll(gate, 2, 0)))
            tails[cols.start] = gate[SUB_OUT - 8:, :]
            if last:
                tail_scr[:, cols] = gate[SUB_OUT - 8:, :]
                cs_ref[:, cols] = gate[SUB_OUT - (CONV_W - 1):, :]
            return g1, g2

        f = _conv_glu(h2, earlier_rows, wup_ref, cw_ref, cb_ref, wdn_ref)
        y_ref[rows, :] = x1 + mod_ref[:, 5 * D:6 * D] * _rms(f, ng_ref[3:4, :])


def _mix_out_prompt(l, x, za, gb, os, lses, mod, norm_g, w_b2d, w_out, w_up, conv_w, conv_b, w_down):
    n, t, _ = x.shape
    tm = TM_OUT
    tok = lambda w: pl.BlockSpec((None, tm, w), lambda i, j: (i, j, 0))
    res = lambda d: pl.BlockSpec((None, d, tm // d, GROUP_W), lambda i, j: (i, 0, j, 0))
    return pl.pallas_call(
        _mix_out_prompt_kernel,
        grid=(n, t // tm),
        in_specs=[
            tok(D), tok(D), tok(D),
            res(DILS[0]), res(DILS[1]), res(DILS[2]), res(DILS[0]), res(DILS[1]), res(DILS[2]),
            pl.BlockSpec((None, 1, 6 * D), lambda i, j: (i, 0, 0)),
            _const_spec((None, 4, D), lambda i, j: (l, 0, 0)),
            _const_spec((None, GROUP_W, D), lambda i, j: (l, 0, 0)),
            _const_spec((None, D, D), lambda i, j: (l, 0, 0)),
            _const_spec((None, D, 2 * D_FF), lambda i, j: (l, 0, 0)),
            _const_spec((None, CONV_W, D_FF), lambda i, j: (l, 0, 0)),
            _const_spec((None, 1, D_FF), lambda i, j: (l, 0, 0)),
            _const_spec((None, D_FF, D), lambda i, j: (l, 0, 0)),
        ],
        out_specs=[tok(D), pl.BlockSpec((None, CONV_W - 1, D_FF), lambda i, j: (i, 0, 0))],
        out_shape=[
            jax.ShapeDtypeStruct((n, t, D), F32),
            jax.ShapeDtypeStruct((n, CONV_W - 1, D_FF), F32),
        ],
        scratch_shapes=[pltpu.VMEM((8, D_FF), F32),
                        pltpu.VMEM((2 * N_GROUPS * GROUP_W // LANES, tm, LANES), F32)],
        compiler_params=_params("arbitrary", "arbitrary"),
        name="mix_out_prompt",
    )(x, za, gb, *os, *lses, mod, norm_g, w_b2d, w_out, w_up, conv_w, conv_b, w_down)


def _mix_out_sample_kernel(x_ref, za_ref, gb_ref, yb_ref, moda_ref, modb_ref, st_ref,
                           ng_ref, wb_ref, wo_ref, wup_ref, cw_ref, cb_ref, wdn_ref,
                           y_ref, gate_ref, fill_scr, *, steps):
    tm = x_ref.shape[0]
    nseq = tm // steps
    slabs = D_FF // LANES
    fill_scr[...] = jnp.zeros_like(fill_scr)
    for s in range(slabs):
        older = st_ref[s, pl.ds(0, nseq, stride=CONV_W - 1), :]
        newer = st_ref[s, pl.ds(1, nseq, stride=CONV_W - 1), :]
        fill_scr[s, pl.ds(0, nseq, stride=steps), :] = newer
        fill_scr[slabs + s, pl.ds(0, nseq, stride=steps), :] = older
        fill_scr[slabs + s, pl.ds(1, nseq, stride=steps), :] = newer

    mods = (moda_ref[:, D:2 * D], modb_ref[:, 0:D], moda_ref[:, 0:D])
    x1, h2 = _mix_out_front(x_ref[...], yb_ref[...].astype(BF16), za_ref[...], gb_ref[...], mods,
                            ng_ref, wb_ref, wo_ref)
    step = lax.broadcasted_iota(jnp.int32, (tm, FF_CHUNK), 0) % steps

    def earlier_rows(cols, gate):
        gate_ref[:, cols] = gate
        first = cols.start // LANES
        fill = lambda base: jnp.concatenate(
            [fill_scr[base + first + s] for s in range(FF_CHUNK // LANES)], axis=-1)
        g1 = jnp.where(step >= 1, pltpu.roll(gate, 1, 0), fill(0))
        g2 = jnp.where(step >= 2, pltpu.roll(gate, 2, 0), fill(slabs))
        return g1, g2

    f = _conv_glu(h2, earlier_rows, wup_ref, cw_ref, cb_ref, wdn_ref)
    y_ref[...] = x1 + modb_ref[:, D:2 * D] * _rms(f, ng_ref[3:4, :])


def _mix_out_sample(l, x, za, gb, yb, mod, state, norm_g, w_b2d, w_out, w_up, conv_w, conv_b,
                    w_down, steps):
    rows = x.shape[0]
    tm = TM_SAMPLE
    slabs = D_FF // LANES
    st_rows = tm // steps * (CONV_W - 1)
    tok = lambda w: pl.BlockSpec((tm, w), lambda i: (i, 0))
    return pl.pallas_call(
        functools.partial(_mix_out_sample_kernel, steps=steps),
        grid=(rows // tm,),
        in_specs=[
            tok(D), tok(D), tok(D), tok(GROUP_W),
            pl.BlockSpec((None, tm, 2 * D), lambda i: (l, i, 1)),
            pl.BlockSpec((None, tm, 2 * D), lambda i: (l, i, 2)),
            pl.BlockSpec((None, slabs, st_rows, LANES), lambda i: (l, 0, i, 0)),
            _const_spec((None, 4, D), lambda i: (l, 0, 0)),
            _const_spec((None, GROUP_W, D), lambda i: (l, 0, 0)),
            _const_spec((None, D, D), lambda i: (l, 0, 0)),
            _const_spec((None, D, 2 * D_FF), lambda i: (l, 0, 0)),
            _const_spec((None, CONV_W, D_FF), lambda i: (l, 0, 0)),
            _const_spec((None, 1, D_FF), lambda i: (l, 0, 0)),
            _const_spec((None, D_FF, D), lambda i: (l, 0, 0)),
        ],
        out_specs=[tok(D), tok(D_FF)],
        out_shape=[
            jax.ShapeDtypeStruct((rows, D), F32),
            jax.ShapeDtypeStruct((rows, D_FF), F32),
        ],
        scratch_shapes=[pltpu.VMEM((2 * slabs, tm, LANES), F32)],
        compiler_params=_params("arbitrary"),
        name="mix_out_sample",
    )(x, za, gb, yb, mod, mod, state, norm_g, w_b2d, w_out, w_up, conv_w, conv_b, w_down)


def _sample_spatial_tables(w_spatial, b_spatial, steps):
    step = jnp.arange(8) % steps
    w_first = w_spatial[:, :, :steps, :steps]
    coefs = []
    for k in range(steps):
        src = step - k
        w = w_first[:, :, step, jnp.maximum(src, 0)]
        coefs.append(jnp.where(src >= 0, w, 0.0))
    c_sp = jnp.stack(coefs, axis=1).transpose(0, 1, 3, 2)
    b_sp = b_spatial[:, :, step].transpose(0, 2, 1)
    return jnp.repeat(c_sp, CHUNK, axis=-1), jnp.repeat(b_sp, CHUNK, axis=-1)


def kernel(x_prompt, x_sample, cache_swa0, cache_swa1, cache_swa2, state_ffn_conv, c_prompt, c_sample,
           ada_w, ada_b, norm_g, w_in, ln_v_g, ln_v_b, w_spatial, b_spatial, w_a2d, w_b2d, w_out,
           w_up, conv_w, conv_b, w_down):
    n_p, t_p, _ = x_prompt.shape
    seqs, steps, _ = x_sample.shape
    rows_s = seqs * steps
    caches = (cache_swa0, cache_swa1, cache_swa2)

    w_in_b = w_in.astype(BF16)
    w_a2d_b, w_b2d_b, w_out_b = w_a2d.astype(BF16), w_b2d.astype(BF16), w_out.astype(BF16)
    w_up_b, w_down_b = w_up.astype(BF16), w_down.astype(BF16)
    ln_g3 = ln_v_g.reshape(DEPTH, 1, A_W)
    ln_b3 = ln_v_b.reshape(DEPTH, 1, A_W)
    conv_b3 = conv_b.reshape(DEPTH, 1, D_FF)
    b_sp = jnp.repeat(b_spatial.transpose(0, 2, 1), CHUNK, axis=-1)
    c_sp_s, b_sp_s = _sample_spatial_tables(w_spatial, b_spatial, steps)

    c_rows = jnp.concatenate([jnp.repeat(c_sample, steps, axis=0), c_prompt], axis=0)
    mod = _modulation(c_rows, ada_w, ada_b)
    mod_p = mod[:, rows_s:].reshape(DEPTH, n_p, 1, 6 * D)

    state_slabs = state_ffn_conv.reshape(DEPTH, seqs * (CONV_W - 1), D_FF // LANES, LANES)
    state_slabs = state_slabs.transpose(0, 2, 1, 3)

    y_p = x_prompt
    y_s = x_sample.reshape(rows_s, D)

    swa_p = [[] for _ in range(N_GROUPS)]
    conv_p, qkv_ss, gate_ss, vn_ss = [], [], [], []
    for l in range(DEPTH):
        za_s, gb_s, qkv_s, vn_s = _mix_in_sample(
            l, y_s, mod, norm_g, w_in_b, ln_g3, ln_b3, c_sp_s, b_sp_s, w_a2d_b, steps)

        za, gb, qkv0, qkv1, qkv2, kv0, kv1, kv2, yb_s = _mix_in_prompt(
            l, y_p, mod_p[l], norm_g, w_in_b, ln_g3, ln_b3, w_spatial, b_sp, w_a2d_b, qkv_s, caches)
        attn = [_band_attn(q, g) for g, q in enumerate((qkv0, qkv1, qkv2))]
        y_p, cst_p = _mix_out_prompt(
            l, y_p, za, gb, [a[0] for a in attn], [a[1] for a in attn], mod_p[l], norm_g,
            w_b2d_b, w_out_b, w_up_b, conv_w, conv_b3, w_down_b)
        for g, kv in enumerate((kv0, kv1, kv2)):
            swa_p[g].append(kv.reshape(n_p, kv.shape[1], 2, HEADS, HEAD_DIM))
        conv_p.append(cst_p)

        y_s, gate_s = _mix_out_sample(
            l, y_s, za_s, gb_s, yb_s, mod, state_slabs, norm_g,
            w_b2d_b, w_out_b, w_up_b, conv_w, conv_b3, w_down_b, steps)
        qkv_ss.append(qkv_s)
        gate_ss.append(gate_s)
        vn_ss.append(vn_s)

    qkv_all = jnp.stack(qkv_ss).reshape(DEPTH, seqs, steps, 3, N_GROUPS, HEADS, HEAD_DIM)
    swa_s = [qkv_all[:, :, :, 1:3, g] for g in range(N_GROUPS)]
    conv_s = jnp.stack(gate_ss).reshape(DEPTH, seqs, steps, D_FF)[:, :, steps - (CONV_W - 1):]
    chunk_v_s = jnp.stack(vn_ss).reshape(DEPTH, seqs, steps, A_W)
    return (y_p, y_s.reshape(seqs, steps, D),
            jnp.stack(swa_p[0]), jnp.stack(swa_p[1]), jnp.stack(swa_p[2]), jnp.stack(conv_p),
            swa_s[0], swa_s[1], swa_s[2], conv_s, chunk_v_s)
```

```python
import functools

import jax
import jax.numpy as jnp
from jax import lax
from jax.experimental import pallas as pl
from jax.experimental.pallas import tpu as pltpu

D = 1024
DEPTH = 4
CHUNK = 128
LANES = 128
A_W = 512
N_GROUPS = 3
DILS = (1, 4, 16)
WINDOWS = (128, 512, 2048)
GROUP_W = 256
HEAD_DIM = 64
HEADS = GROUP_W // HEAD_DIM
QKV_G = 3 * GROUP_W
QKV_W = N_GROUPS * QKV_G
D_FF = 2816
IN_W = 2 * A_W + QKV_W + 2 * D
O_QKV = 2 * A_W
O_GATE = O_QKV + QKV_W
CONV_W = 3
EPS = 1e-6
NEG = -1e30
Q_SCALE = HEAD_DIM ** -0.5
LOG2_E = 1.4426950408889634
LN_2 = 0.6931471805599453

TM = 256
TM_OUT = 512
SUB_OUT = 256
COL_CHUNK = 256
FF_CHUNK = D_FF
TM_SAMPLE = 128
VMEM_LIMIT = 56 * 1024 * 1024

F32 = jnp.float32
BF16 = jnp.bfloat16
_NT = (((1,), (1,)), ((), ()))


def _dot(a, b):
    return jnp.dot(a, b, preferred_element_type=F32)


def _dot_nt(a, b):
    return lax.dot_general(a, b, _NT, preferred_element_type=F32)


def _rms(x, g):
    return x * lax.rsqrt(jnp.mean(x * x, axis=-1, keepdims=True) + EPS) * g


def _gelu(x):
    return jax.nn.gelu(x)


def _layer_norm(x, g, b):
    mu = jnp.mean(x, axis=-1, keepdims=True)
    xc = x - mu
    var = jnp.mean(xc * xc, axis=-1, keepdims=True)
    return xc * lax.rsqrt(var + EPS) * g + b


def _const_spec(shape, index_map):
    return pl.BlockSpec(shape, index_map, pipeline_mode=pl.Buffered(1))


def _params(*sem):
    return pltpu.CompilerParams(dimension_semantics=sem, vmem_limit_bytes=VMEM_LIMIT)


def _head_of_lane():
    return lax.broadcasted_iota(jnp.int32, (1, GROUP_W), 1) // HEAD_DIM


def _mod_kernel(c_ref, w_ref, b_ref, o_ref):
    c = c_ref[...]
    s = (c * jax.nn.sigmoid(c)).astype(BF16)
    o_ref[...] = _dot(s, w_ref[...].astype(BF16)) + b_ref[...]


def _modulation(c_all, ada_w, ada_b):
    rows = c_all.shape[0]
    tn = 1536
    return pl.pallas_call(
        _mod_kernel,
        grid=(DEPTH, 6 * D // tn),
        in_specs=[
            pl.BlockSpec((rows, D), lambda l, j: (0, 0)),
            pl.BlockSpec((None, D, tn), lambda l, j: (l, 0, j)),
            pl.BlockSpec((None, 1, tn), lambda l, j: (l, 0, j)),
        ],
        out_specs=pl.BlockSpec((None, rows, tn), lambda l, j: (l, 0, j)),
        out_shape=jax.ShapeDtypeStruct((DEPTH, rows, 6 * D), F32),
        compiler_params=_params("arbitrary", "arbitrary"),
        name="adaln_mod",
    )(c_all, ada_w, ada_b.reshape(DEPTH, 1, 6 * D))


def _project_in(x, sh1, sc1, ng_ref, win_ref, lng_ref, lnb_ref):
    h = (_rms(x, ng_ref[0:1, :]) * (1.0 + sc1) + sh1).astype(BF16)
    puv = _dot(h, win_ref[:, 0:O_QKV])
    gu = _gelu(puv[:, :A_W])
    vn = _layer_norm(_gelu(puv[:, A_W:]), lng_ref[...], lnb_ref[...])
    return h, gu, vn


def _gates_out(h, ya, win_ref, wa_ref, za_ref, gb_ref):
    for c in range(D // COL_CHUNK):
        cols = slice(c * COL_CHUNK, (c + 1) * COL_CHUNK)
        ga = jax.nn.sigmoid(_dot(h, win_ref[:, O_GATE + c * COL_CHUNK:O_GATE + (c + 1) * COL_CHUNK]))
        za_ref[:, cols] = (ga * _dot(ya, wa_ref[:, cols])).astype(za_ref.dtype)
        o = O_GATE + D + c * COL_CHUNK
        gb_ref[:, cols] = jax.nn.sigmoid(_dot(h, win_ref[:, o:o + COL_CHUNK])).astype(gb_ref.dtype)


def _qkv_col(which, g):
    return (which * N_GROUPS + g) * GROUP_W


def _mix_in_prompt_kernel(x_ref, mod_ref, ng_ref, win_ref, lng_ref, lnb_ref, ws_ref, bsp_ref, wa_ref,
                          qkvs_ref, c0_ref, c1_ref, c2_ref, *rest):
    (za_ref, gb_ref, qkv0_ref, qkv1_ref, qkv2_ref, kv0_ref, kv1_ref, kv2_ref, ys_ref,
     ya_scr, perm_scr) = rest[-11:]
    _cache_attn_body(qkvs_ref, (c0_ref, c1_ref, c2_ref), ys_ref)
    tm = x_ref.shape[0]
    h, gu, vn = _project_in(x_ref[...], mod_ref[:, 0:D], mod_ref[:, D:2 * D],
                            ng_ref, win_ref, lng_ref, lnb_ref)
    vnb = vn.astype(BF16)
    ti = lax.broadcasted_iota(jnp.int32, (CHUNK, CHUNK), 0)
    si = lax.broadcasted_iota(jnp.int32, (CHUNK, CHUNK), 1)
    causal = si <= ti
    for g in range(A_W // CHUNK):
        wg = jnp.where(causal, ws_ref[g], 0.0).astype(BF16)
        for c in range(tm // CHUNK):
            rows = slice(c * CHUNK, (c + 1) * CHUNK)
            cols = slice(g * CHUNK, (g + 1) * CHUNK)
            mix = _dot(wg, vnb[rows, cols]) + bsp_ref[:, cols]
            ya_scr[rows, cols] = (gu[rows, cols] * mix).astype(BF16)
    _gates_out(h, ya_scr[...], win_ref, wa_ref, za_ref, gb_ref)

    pq = _dot(h, win_ref[:, O_QKV:O_GATE])
    for g, kv_ref in enumerate((kv0_ref, kv1_ref, kv2_ref)):
        first = tm - kv_ref.shape[-2]
        for which in (1, 2):
            col = _qkv_col(which, g)
            cols = slice((which - 1) * GROUP_W, which * GROUP_W)
            if len(kv_ref.shape) == 2:
                kv_ref[:, cols] = pq[first:, col:col + GROUP_W]
            else:
                for layer in range(kv_ref.shape[0]):
                    kv_ref[layer, :, cols] = pq[first:, col:col + GROUP_W]
    halves = GROUP_W // LANES
    for g, out_ref in enumerate((qkv0_ref, qkv1_ref, qkv2_ref)):
        dil = DILS[g]
        for which in range(3):
            scale = Q_SCALE * LOG2_E if which == 0 else 1.0
            col = _qkv_col(which, g)
            if dil == 1:
                out_ref[0, :, which * GROUP_W:(which + 1) * GROUP_W] = (
                    pq[:, col:col + GROUP_W] * scale).astype(BF16)
                continue
            for s in range(halves):
                perm_scr[which * halves + s] = pq[:, col + s * LANES:col + (s + 1) * LANES] * scale
        if dil == 1:
            continue
        for r in range(dil):
            for s in range(3 * halves):
                out_ref[r, :, s * LANES:(s + 1) * LANES] = (
                    perm_scr[s, pl.ds(r, tm // dil, stride=dil), :].astype(BF16))


def _mix_in_prompt(l, x, mod, norm_g, w_in, ln_g, ln_b, w_sp, b_sp, w_a2d, qkv_s, caches, kv_stacks):
    n, t, _ = x.shape
    steps = t // TM
    keep = [min(w, t) for w in WINDOWS]
    kv_rows = [min(k, TM) for k in keep]
    first = [(t - k) // TM for k in keep]
    seqs = caches[0].shape[1]
    steps_s = qkv_s.shape[0] // seqs
    nb = seqs // (n * steps)
    rows_s = nb * steps_s
    assert nb * n * steps == seqs and rows_s % 8 == 0
    views = _cache_views(caches, steps_s)

    def kv_spec(g):
        layers = DEPTH if l == 0 else None
        return pl.BlockSpec((layers, None, kv_rows[g], 2 * GROUP_W),
                            lambda i, j, g=g: (0 if l == 0 else l, i, jnp.maximum(j - first[g], 0), 0))

    n_in = 13
    aliased = [] if l == 0 else list(kv_stacks)

    def qkv_spec(g):
        return pl.BlockSpec((None, DILS[g], TM // DILS[g], QKV_G), lambda i, j: (i, 0, j, 0))

    tok = lambda w: pl.BlockSpec((None, TM, w), lambda i, j: (i, j, 0))
    return pl.pallas_call(
        _mix_in_prompt_kernel,
        grid=(n, steps),
        in_specs=[
            tok(D),
            pl.BlockSpec((None, 1, 6 * D), lambda i, j: (i, 0, 0)),
            _const_spec((None, 4, D), lambda i, j: (l, 0, 0)),
            _const_spec((None, D, IN_W), lambda i, j: (l, 0, 0)),
            _const_spec((None, 1, A_W), lambda i, j: (l, 0, 0)),
            _const_spec((None, 1, A_W), lambda i, j: (l, 0, 0)),
            _const_spec((None, 4, CHUNK, CHUNK), lambda i, j: (l, 0, 0, 0)),
            _const_spec((None, CHUNK, A_W), lambda i, j: (l, 0, 0)),
            _const_spec((None, A_W, D), lambda i, j: (l, 0, 0)),
            pl.BlockSpec((rows_s, QKV_W), lambda i, j: (i * steps + j, 0)),
        ] + [
            pl.BlockSpec((None, nb, 2 * GROUP_W, v.shape[3]), lambda i, j: (l, i * steps + j, 0, 0))
            for v in views
        ] + [pl.BlockSpec(memory_space=pl.ANY) for _ in aliased],
        input_output_aliases={n_in + g: 5 + g for g in range(len(aliased))},
        out_specs=[tok(D), tok(D), qkv_spec(0), qkv_spec(1), qkv_spec(2),
                   kv_spec(0), kv_spec(1), kv_spec(2),
                   pl.BlockSpec((rows_s, GROUP_W), lambda i, j: (i * steps + j, 0))],
        out_shape=[
            jax.ShapeDtypeStruct((n, t, D), BF16),
            jax.ShapeDtypeStruct((n, t, D), BF16),
        ] + [jax.ShapeDtypeStruct((n, d, t // d, QKV_G), BF16) for d in DILS] + [
            jax.ShapeDtypeStruct((DEPTH, n, k, 2 * GROUP_W), F32) for k in keep
        ] + [jax.ShapeDtypeStruct((seqs * steps_s, GROUP_W), F32)],
        scratch_shapes=[pltpu.VMEM((TM, A_W), BF16), pltpu.VMEM((QKV_G // LANES, TM, LANES), F32)],
        compiler_params=_params("arbitrary", "arbitrary"),
        name="mix_in_prompt",
    )(x, mod, norm_g, w_in, ln_g, ln_b, w_sp, b_sp, w_a2d, qkv_s, *views, *aliased)


def _mix_in_sample_kernel(x_ref, mod_ref, ng_ref, win_ref, lng_ref, lnb_ref, csp_ref, bsp_ref, wa_ref,
                          za_ref, gb_ref, qkv_ref, vn_ref, *, steps):
    tm = x_ref.shape[0]
    h, gu, vn = _project_in(x_ref[...], mod_ref[:, 0:D], mod_ref[:, D:2 * D],
                            ng_ref, win_ref, lng_ref, lnb_ref)
    vn_ref[...] = vn
    mix = jnp.zeros((tm // 8, 8, A_W), F32) + bsp_ref[...][None]
    for k in range(steps):
        prev = vn if k == 0 else pltpu.roll(vn, k, 0)
        mix = mix + csp_ref[k][None] * prev.reshape(tm // 8, 8, A_W)
    ya = (gu * mix.reshape(tm, A_W)).astype(BF16)
    _gates_out(h, ya, win_ref, wa_ref, za_ref, gb_ref)
    qkv_ref[...] = _dot(h, win_ref[:, O_QKV:O_GATE])


def _mix_in_sample(l, x, mod, norm_g, w_in, ln_g, ln_b, c_sp, b_sp, w_a2d, steps):
    rows = x.shape[0]
    tm = TM_SAMPLE
    tok = lambda w: pl.BlockSpec((tm, w), lambda i: (i, 0))
    return pl.pallas_call(
        functools.partial(_mix_in_sample_kernel, steps=steps),
        grid=(rows // tm,),
        in_specs=[
            tok(D),
            pl.BlockSpec((None, tm, 2 * D), lambda i: (l, i, 0)),
            _const_spec((None, 4, D), lambda i: (l, 0, 0)),
            _const_spec((None, D, IN_W), lambda i: (l, 0, 0)),
            _const_spec((None, 1, A_W), lambda i: (l, 0, 0)),
            _const_spec((None, 1, A_W), lambda i: (l, 0, 0)),
            _const_spec((None, steps, 8, A_W), lambda i: (l, 0, 0, 0)),
            _const_spec((None, 8, A_W), lambda i: (l, 0, 0)),
            _const_spec((None, A_W, D), lambda i: (l, 0, 0)),
        ],
        out_specs=[tok(D), tok(D), tok(QKV_W), tok(A_W)],
        out_shape=[
            jax.ShapeDtypeStruct((rows, D), BF16),
            jax.ShapeDtypeStruct((rows, D), BF16),
            jax.ShapeDtypeStruct((rows, QKV_W), F32),
            jax.ShapeDtypeStruct((rows, A_W), F32),
        ],
        compiler_params=_params("arbitrary"),
        name="mix_in_sample",
    )(x, mod, norm_g, w_in, ln_g, ln_b, c_sp, b_sp, w_a2d)


def _band_attn_kernel(qkv_ref, o_ref, lse_ref, *, span):
    n_res, length, _ = qkv_ref.shape
    nb = length // CHUNK
    kw = CHUNK if nb == 1 else 2 * CHUNK
    head = _head_of_lane()
    qi = lax.broadcasted_iota(jnp.int32, (CHUNK, kw), 0)
    ki = lax.broadcasted_iota(jnp.int32, (CHUNK, kw), 1)
    ones = jnp.ones((kw, LANES), BF16)

    def unit(u, carry):
        r = u // nb
        b = u % nb
        q0 = pl.multiple_of(b * CHUNK, CHUNK)
        k0 = pl.multiple_of(jnp.maximum(b - 1, 0) * CHUNK, CHUNK)
        q = qkv_ref[r, pl.ds(q0, CHUNK), 0:GROUP_W]
        k = qkv_ref[r, pl.ds(k0, kw), GROUP_W:2 * GROUP_W]
        v = qkv_ref[r, pl.ds(k0, kw), 2 * GROUP_W:QKV_G]
        dist = (q0 - k0) + qi - ki
        valid = (dist >= 0) & (dist <= span)
        qs = jnp.concatenate([jnp.where(head == hh, q, jnp.zeros_like(q)) for hh in range(HEADS)], axis=0)
        s = _dot_nt(qs, k).reshape(HEADS, CHUNK, kw)
        s = jnp.where(valid[None], s, NEG).reshape(HEADS * CHUNK, kw)
        m = jnp.max(s, axis=-1, keepdims=True)
        p = jnp.exp2(s - m).astype(BF16)
        z = _dot(p, ones)
        z = jnp.concatenate([z, z], axis=-1)
        o = _dot(p, v)
        acc, zacc = o[0:CHUNK], z[0:CHUNK]
        macc = jnp.broadcast_to(m[0:CHUNK], (CHUNK, GROUP_W))
        for hh in range(1, HEADS):
            rows = slice(hh * CHUNK, (hh + 1) * CHUNK)
            acc = jnp.where(head == hh, o[rows], acc)
            zacc = jnp.where(head == hh, z[rows], zacc)
            macc = jnp.where(head == hh, m[rows], macc)
        o_ref[r, pl.ds(q0, CHUNK), :] = (acc / zacc).astype(o_ref.dtype)
        lse_ref[r, pl.ds(q0, CHUNK), :] = (macc + jnp.log2(zacc)) * LN_2
        return carry

    lax.fori_loop(0, n_res * nb, unit, 0, unroll=8)


def _band_attn(qkv, g):
    n, dil, length, _ = qkv.shape
    span = WINDOWS[g] // dil
    out_spec = pl.BlockSpec((None, dil, length, GROUP_W), lambda i: (i, 0, 0, 0))
    return pl.pallas_call(
        functools.partial(_band_attn_kernel, span=span),
        grid=(n,),
        in_specs=[pl.BlockSpec((None, dil, length, QKV_G), lambda i: (i, 0, 0, 0))],
        out_specs=[out_spec, out_spec],
        out_shape=[
            jax.ShapeDtypeStruct((n, dil, length, GROUP_W), BF16),
            jax.ShapeDtypeStruct((n, dil, length, GROUP_W), F32),
        ],
        compiler_params=_params("arbitrary"),
        name=f"band_attn_g{g}",
    )(qkv)


def _cached_group(q, k_new, v_new, cache_ref, dil):
    steps = q.shape[0]
    past = cache_ref.shape[1]
    head = _head_of_lane()
    q8 = jnp.concatenate([q * Q_SCALE, jnp.zeros((8 - steps, GROUP_W), F32)], axis=0).astype(BF16)
    qs = jnp.concatenate([jnp.where(head == hh, q8, jnp.zeros_like(q8)) for hh in range(HEADS)], axis=0)
    kt = cache_ref[0:GROUP_W, :].astype(BF16)
    vt = cache_ref[GROUP_W:2 * GROUP_W, :].astype(BF16)
    rows = HEADS * 8
    step_c = lax.broadcasted_iota(jnp.int32, (rows, past), 0) % steps
    pos = lax.broadcasted_iota(jnp.int32, (rows, past), 1)
    step_n = lax.broadcasted_iota(jnp.int32, (rows, steps), 0) % steps
    new = lax.broadcasted_iota(jnp.int32, (rows, steps), 1)
    if dil == 1:
        valid_c = pos >= step_c
        valid_n = new <= step_n
    else:
        valid_c = pos % dil == step_c
        valid_n = new == step_n
    s_c = jnp.where(valid_c, _dot(qs, kt), NEG)
    s_n = jnp.where(valid_n, _dot_nt(qs, k_new.astype(BF16)), NEG)
    m = jnp.maximum(jnp.max(s_c, axis=-1, keepdims=True), jnp.max(s_n, axis=-1, keepdims=True))
    p_c = jnp.exp(s_c - m)
    p_n = jnp.exp(s_n - m)
    z = jnp.sum(p_c, axis=-1, keepdims=True) + jnp.sum(p_n, axis=-1, keepdims=True)
    o = (_dot_nt(p_c.astype(BF16), vt) + _dot(p_n.astype(BF16), v_new.astype(BF16))) / z
    lse = jnp.broadcast_to(m + jnp.log(z), (rows, GROUP_W))
    acc = o[0:8]
    lacc = lse[0:8]
    for hh in range(1, HEADS):
        acc = jnp.where(head == hh, o[hh * 8:(hh + 1) * 8], acc)
        lacc = jnp.where(head == hh, lse[hh * 8:(hh + 1) * 8], lacc)
    return acc, lacc


def _cache_attn_body(qkv_ref, cache_refs, y_ref):
    nseq = cache_refs[0].shape[0]
    steps = qkv_ref.shape[0] // nseq
    for i in range(nseq):
        rows = slice(i * steps, (i + 1) * steps)
        outs = []
        for g, c_ref in enumerate(cache_refs):
            q, k_new, v_new = (qkv_ref[rows, _qkv_col(which, g):_qkv_col(which, g) + GROUP_W]
                               for which in range(3))
            outs.append(_cached_group(q, k_new, v_new, c_ref.at[i], DILS[g]))
        y_ref[rows, :] = _merge(outs)[0:steps]


def _merge(groups):
    lmax = groups[0][1]
    for _, ls in groups[1:]:
        lmax = jnp.maximum(lmax, ls)
    num = 0.0
    den = 0.0
    for o, ls in groups:
        e = jnp.exp(ls - lmax)
        num = num + e * o
        den = den + e
    return num / den


def _cache_views(caches, steps):
    views = []
    for g, c in enumerate(caches):
        depth, seqs, past = c.shape[:3]
        assert past == WINDOWS[g] and steps <= DILS[1], "window fully cached; steps fit one residue period"
        views.append(c.transpose(0, 1, 3, 4, 5, 2).reshape(depth, seqs, 2 * GROUP_W, past))
    return views


def _mix_out_front(x, yb, za, gb, mods, ng_ref, wb_ref, wo_ref):
    sh2, sc2, gt1 = mods
    merged = za.astype(F32) + gb.astype(F32) * _dot(yb, wb_ref[...])
    x1 = x + gt1 * _rms(_dot(merged.astype(BF16), wo_ref[...]), ng_ref[1:2, :])
    h2 = (_rms(x1, ng_ref[2:3, :]) * (1.0 + sc2) + sh2).astype(BF16)
    return x1, h2


def _conv_glu(h2, earlier_rows, wup_ref, cw_ref, cb_ref, wdn_ref):
    f = None
    for c in range(D_FF // FF_CHUNK):
        cols = slice(c * FF_CHUNK, (c + 1) * FF_CHUNK)
        gate = _dot(h2, wup_ref[:, cols])
        val = _dot(h2, wup_ref[:, D_FF + c * FF_CHUNK:D_FF + (c + 1) * FF_CHUNK])
        g1, g2 = earlier_rows(cols, gate)
        conv = cb_ref[:, cols] + cw_ref[0:1, cols] * g2 + cw_ref[1:2, cols] * g1 + cw_ref[2:3, cols] * gate
        part = _dot((_gelu(conv) * val).astype(BF16), wdn_ref[cols, :])
        f = part if f is None else f + part
    return f


def _interleave(blk_ref, scr, base, tm):
    dil = blk_ref.shape[0]
    if dil == 1:
        return blk_ref[0].astype(F32)
    for r in range(dil):
        x = blk_ref[r].astype(F32)
        for s in range(GROUP_W // LANES):
            scr[base + s, pl.ds(r, tm // dil, stride=dil), :] = x[:, s * LANES:(s + 1) * LANES]
    return jnp.concatenate([scr[base + s] for s in range(GROUP_W // LANES)], axis=-1)


def _mix_out_prompt_kernel(x_ref, za_ref, gb_ref, o0_ref, o1_ref, o2_ref, l0_ref, l1_ref, l2_ref,
                           mod_ref, ng_ref, wb_ref, wo_ref, wup_ref, cw_ref, cb_ref, wdn_ref,
                           y_ref, cs_ref, tail_scr, perm_scr):
    tm = x_ref.shape[0]

    @pl.when(pl.program_id(1) == 0)
    def _():
        tail_scr[...] = jnp.zeros_like(tail_scr)

    slabs = GROUP_W // LANES
    groups = []
    for g, (o_ref, l_ref) in enumerate(((o0_ref, l0_ref), (o1_ref, l1_ref), (o2_ref, l2_ref))):
        groups.append((_interleave(o_ref, perm_scr, 2 * g * slabs, tm),
                       _interleave(l_ref, perm_scr, (2 * g + 1) * slabs, tm)))
    yb = _merge(groups).astype(BF16)
    mods = (mod_ref[:, 3 * D:4 * D], mod_ref[:, 4 * D:5 * D], mod_ref[:, 2 * D:3 * D])
    row = lax.broadcasted_iota(jnp.int32, (SUB_OUT, FF_CHUNK), 0)
    tails = {}
    n_sub = tm // SUB_OUT
    for h in range(n_sub):
        rows = slice(h * SUB_OUT, (h + 1) * SUB_OUT)
        x1, h2 = _mix_out_front(x_ref[rows, :], yb[rows, :], za_ref[rows, :], gb_ref[rows, :], mods,
                                ng_ref, wb_ref, wo_ref)

        def earlier_rows(cols, gate, first=(h == 0), last=(h == n_sub - 1)):
            prev = tail_scr[:, cols] if first else tails[cols.start]
            g1 = jnp.where(row == 0, prev[7:8, :], pltpu.roll(gate, 1, 0))
            g2 = jnp.where(row == 0, prev[6:7, :],
                           jnp.where(row == 1, prev[7:8, :], pltpu.roll(gate, 2, 0)))
            tails[cols.start] = gate[SUB_OUT - 8:, :]
            if last:
                tail_scr[:, cols] = gate[SUB_OUT - 8:, :]
                cs_ref[:, cols] = gate[SUB_OUT - (CONV_W - 1):, :]
            return g1, g2

        f = _conv_glu(h2, earlier_rows, wup_ref, cw_ref, cb_ref, wdn_ref)
        y_ref[rows, :] = x1 + mod_ref[:, 5 * D:6 * D] * _rms(f, ng_ref[3:4, :])


def _mix_out_prompt(l, x, za, gb, os, lses, mod, norm_g, w_b2d, w_out, w_up, conv_w, conv_b, w_down):
    n, t, _ = x.shape
    tm = TM_OUT
    tok = lambda w: pl.BlockSpec((None, tm, w), lambda i, j: (i, j, 0))
    res = lambda d: pl.BlockSpec((None, d, tm // d, GROUP_W), lambda i, j: (i, 0, j, 0))
    return pl.pallas_call(
        _mix_out_prompt_kernel,
        grid=(n, t // tm),
        in_specs=[
            tok(D), tok(D), tok(D),
            res(DILS[0]), res(DILS[1]), res(DILS[2]), res(DILS[0]), res(DILS[1]), res(DILS[2]),
            pl.BlockSpec((None, 1, 6 * D), lambda i, j: (i, 0, 0)),
            _const_spec((None, 4, D), lambda i, j: (l, 0, 0)),
            _const_spec((None, GROUP_W, D), lambda i, j: (l, 0, 0)),
            _const_spec((None, D, D), lambda i, j: (l, 0, 0)),
            _const_spec((None, D, 2 * D_FF), lambda i, j: (l, 0, 0)),
            _const_spec((None, CONV_W, D_FF), lambda i, j: (l, 0, 0)),
            _const_spec((None, 1, D_FF), lambda i, j: (l, 0, 0)),
            _const_spec((None, D_FF, D), lambda i, j: (l, 0, 0)),
        ],
        out_specs=[tok(D), pl.BlockSpec((None, CONV_W - 1, D_FF), lambda i, j: (i, 0, 0))],
        out_shape=[
            jax.ShapeDtypeStruct((n, t, D), F32),
            jax.ShapeDtypeStruct((n, CONV_W - 1, D_FF), F32),
        ],
        scratch_shapes=[pltpu.VMEM((8, D_FF), F32),
                        pltpu.VMEM((2 * N_GROUPS * GROUP_W // LANES, tm, LANES), F32)],
        compiler_params=_params("arbitrary", "arbitrary"),
        name="mix_out_prompt",
    )(x, za, gb, *os, *lses, mod, norm_g, w_b2d, w_out, w_up, conv_w, conv_b, w_down)


def _mix_out_sample_kernel(x_ref, za_ref, gb_ref, yb_ref, moda_ref, modb_ref, st_ref,
                           ng_ref, wb_ref, wo_ref, wup_ref, cw_ref, cb_ref, wdn_ref,
                           y_ref, gate_ref, fill_scr, *, steps):
    tm = x_ref.shape[0]
    nseq = tm // steps
    slabs = D_FF // LANES
    fill_scr[...] = jnp.zeros_like(fill_scr)
    for s in range(slabs):
        older = st_ref[s, pl.ds(0, nseq, stride=CONV_W - 1), :]
        newer = st_ref[s, pl.ds(1, nseq, stride=CONV_W - 1), :]
        fill_scr[s, pl.ds(0, nseq, stride=steps), :] = newer
        fill_scr[slabs + s, pl.ds(0, nseq, stride=steps), :] = older
        fill_scr[slabs + s, pl.ds(1, nseq, stride=steps), :] = newer

    mods = (moda_ref[:, D:2 * D], modb_ref[:, 0:D], moda_ref[:, 0:D])
    x1, h2 = _mix_out_front(x_ref[...], yb_ref[...].astype(BF16), za_ref[...], gb_ref[...], mods,
                            ng_ref, wb_ref, wo_ref)
    step = lax.broadcasted_iota(jnp.int32, (tm, FF_CHUNK), 0) % steps

    def earlier_rows(cols, gate):
        gate_ref[:, cols] = gate
        first = cols.start // LANES
        fill = lambda base: jnp.concatenate(
            [fill_scr[base + first + s] for s in range(FF_CHUNK // LANES)], axis=-1)
        g1 = jnp.where(step >= 1, pltpu.roll(gate, 1, 0), fill(0))
        g2 = jnp.where(step >= 2, pltpu.roll(gate, 2, 0), fill(slabs))
        return g1, g2

    f = _conv_glu(h2, earlier_rows, wup_ref, cw_ref, cb_ref, wdn_ref)
    y_ref[...] = x1 + modb_ref[:, D:2 * D] * _rms(f, ng_ref[3:4, :])


def _mix_out_sample(l, x, za, gb, yb, mod, state, norm_g, w_b2d, w_out, w_up, conv_w, conv_b,
                    w_down, steps):
    rows = x.shape[0]
    tm = TM_SAMPLE
    slabs = D_FF // LANES
    st_rows = tm // steps * (CONV_W - 1)
    tok = lambda w: pl.BlockSpec((tm, w), lambda i: (i, 0))
    return pl.pallas_call(
        functools.partial(_mix_out_sample_kernel, steps=steps),
        grid=(rows // tm,),
        in_specs=[
            tok(D), tok(D), tok(D), tok(GROUP_W),
            pl.BlockSpec((None, tm, 2 * D), lambda i: (l, i, 1)),
            pl.BlockSpec((None, tm, 2 * D), lambda i: (l, i, 2)),
            pl.BlockSpec((None, slabs, st_rows, LANES), lambda i: (l, 0, i, 0)),
            _const_spec((None, 4, D), lambda i: (l, 0, 0)),
            _const_spec((None, GROUP_W, D), lambda i: (l, 0, 0)),
            _const_spec((None, D, D), lambda i: (l, 0, 0)),
            _const_spec((None, D, 2 * D_FF), lambda i: (l, 0, 0)),
            _const_spec((None, CONV_W, D_FF), lambda i: (l, 0, 0)),
            _const_spec((None, 1, D_FF), lambda i: (l, 0, 0)),
            _const_spec((None, D_FF, D), lambda i: (l, 0, 0)),
        ],
        out_specs=[tok(D), tok(D_FF)],
        out_shape=[
            jax.ShapeDtypeStruct((rows, D), F32),
            jax.ShapeDtypeStruct((rows, D_FF), F32),
        ],
        scratch_shapes=[pltpu.VMEM((2 * slabs, tm, LANES), F32)],
        compiler_params=_params("arbitrary"),
        name="mix_out_sample",
    )(x, za, gb, yb, mod, mod, state, norm_g, w_b2d, w_out, w_up, conv_w, conv_b, w_down)


def _sample_spatial_tables(w_spatial, b_spatial, steps):
    step = jnp.arange(8) % steps
    w_first = w_spatial[:, :, :steps, :steps]
    coefs = []
    for k in range(steps):
        src = step - k
        w = w_first[:, :, step, jnp.maximum(src, 0)]
        coefs.append(jnp.where(src >= 0, w, 0.0))
    c_sp = jnp.stack(coefs, axis=1).transpose(0, 1, 3, 2)
    b_sp = b_spatial[:, :, step].transpose(0, 2, 1)
    return jnp.repeat(c_sp, CHUNK, axis=-1), jnp.repeat(b_sp, CHUNK, axis=-1)


def kernel(x_prompt, x_sample, cache_swa0, cache_swa1, cache_swa2, state_ffn_conv, c_prompt, c_sample,
           ada_w, ada_b, norm_g, w_in, ln_v_g, ln_v_b, w_spatial, b_spatial, w_a2d, w_b2d, w_out,
           w_up, conv_w, conv_b, w_down):
    n_p, t_p, _ = x_prompt.shape
    seqs, steps, _ = x_sample.shape
    rows_s = seqs * steps
    caches = (cache_swa0, cache_swa1, cache_swa2)

    w_in_b = w_in.astype(BF16)
    w_a2d_b, w_b2d_b, w_out_b = w_a2d.astype(BF16), w_b2d.astype(BF16), w_out.astype(BF16)
    w_up_b, w_down_b = w_up.astype(BF16), w_down.astype(BF16)
    ln_g3 = ln_v_g.reshape(DEPTH, 1, A_W)
    ln_b3 = ln_v_b.reshape(DEPTH, 1, A_W)
    conv_b3 = conv_b.reshape(DEPTH, 1, D_FF)
    b_sp = jnp.repeat(b_spatial.transpose(0, 2, 1), CHUNK, axis=-1)
    c_sp_s, b_sp_s = _sample_spatial_tables(w_spatial, b_spatial, steps)

    c_rows = jnp.concatenate([jnp.repeat(c_sample, steps, axis=0), c_prompt], axis=0)
    mod = _modulation(c_rows, ada_w, ada_b)
    mod_p = mod[:, rows_s:].reshape(DEPTH, n_p, 1, 6 * D)

    state_slabs = state_ffn_conv.reshape(DEPTH, seqs * (CONV_W - 1), D_FF // LANES, LANES)
    state_slabs = state_slabs.transpose(0, 2, 1, 3)

    y_p = x_prompt
    y_s = x_sample.reshape(rows_s, D)

    kv_stacks = None
    conv_p, qkv_ss, gate_ss, vn_ss = [], [], [], []
    for l in range(DEPTH):
        za_s, gb_s, qkv_s, vn_s = _mix_in_sample(
            l, y_s, mod, norm_g, w_in_b, ln_g3, ln_b3, c_sp_s, b_sp_s, w_a2d_b, steps)

        za, gb, qkv0, qkv1, qkv2, kv0, kv1, kv2, yb_s = _mix_in_prompt(
            l, y_p, mod_p[l], norm_g, w_in_b, ln_g3, ln_b3, w_spatial, b_sp, w_a2d_b, qkv_s, caches,
            kv_stacks)
        kv_stacks = (kv0, kv1, kv2)
        attn = [_band_attn(q, g) for g, q in enumerate((qkv0, qkv1, qkv2))]
        y_p, cst_p = _mix_out_prompt(
            l, y_p, za, gb, [a[0] for a in attn], [a[1] for a in attn], mod_p[l], norm_g,
            w_b2d_b, w_out_b, w_up_b, conv_w, conv_b3, w_down_b)
        conv_p.append(cst_p)

        y_s, gate_s = _mix_out_sample(
            l, y_s, za_s, gb_s, yb_s, mod, state_slabs, norm_g,
            w_b2d_b, w_out_b, w_up_b, conv_w, conv_b3, w_down_b, steps)
        qkv_ss.append(qkv_s)
        gate_ss.append(gate_s)
        vn_ss.append(vn_s)

    qkv_all = jnp.stack(qkv_ss).reshape(DEPTH, seqs, steps, 3, N_GROUPS, HEADS, HEAD_DIM)
    swa_s = [qkv_all[:, :, :, 1:3, g] for g in range(N_GROUPS)]
    conv_s = jnp.stack(gate_ss).reshape(DEPTH, seqs, steps, D_FF)[:, :, steps - (CONV_W - 1):]
    chunk_v_s = jnp.stack(vn_ss).reshape(DEPTH, seqs, steps, A_W)
    swa_p = [kv.reshape(DEPTH, n_p, kv.shape[2], 2, HEADS, HEAD_DIM) for kv in kv_stacks]
    return (y_p, y_s.reshape(seqs, steps, D),
            swa_p[0], swa_p[1], swa_p[2], jnp.stack(conv_p),
            swa_s[0], swa_s[1], swa_s[2], conv_s, chunk_v_s)
```
